```python
import jax, jax.numpy as jnp
from jax import lax
import numpy as np

D_MODEL = 4096
BATCH = 4
SEQ = 4096
DEPTH = 1

MLA_HEADS = 16
MLA_Q_RANK = 1024
MLA_KV_RANK = 512
MLA_NOPE = 128
MLA_ROPE = 64
MLA_V = 128
MLA_QK = MLA_NOPE + MLA_ROPE
ROPE_THETA = 10000.0
SWA_Q_HEADS = 32
SWA_KV_HEADS = 8
SWA_HEAD_DIM = 64
SWA_GROUP = SWA_Q_HEADS // SWA_KV_HEADS
SWA_WINDOW = 128
BLOCK = 128
MIX_A = MLA_HEADS * MLA_V
MIX_B = SWA_Q_HEADS * SWA_HEAD_DIM
MIX_WIDTH = MIX_A + MIX_B
D_FF = -(-8 * D_MODEL // (3 * 256)) * 256
EPS = 1e-6
IN_SPLITS = (MLA_Q_RANK, MLA_KV_RANK, MLA_ROPE, MIX_B, SWA_KV_HEADS * SWA_HEAD_DIM, SWA_KV_HEADS * SWA_HEAD_DIM)
IN_WIDTH = sum(IN_SPLITS)
SPLIT_IDX = tuple(int(v) for v in np.cumsum(IN_SPLITS)[:-1])

kernel_name = "hybrid_mla_swa_sink_alibi_swiglu_sandwich"


def rms_norm(x, g):
    xf = x.astype(jnp.float32)
    y = xf * lax.rsqrt(jnp.mean(xf * xf, axis=-1, keepdims=True) + EPS)
    return (y * g.astype(jnp.float32)).astype(x.dtype)


def rope_tables(positions):
    inv = 1.0 / (ROPE_THETA ** (jnp.arange(0, MLA_ROPE, 2, dtype=jnp.float32) / MLA_ROPE))
    ang = positions.astype(jnp.float32)[..., None] * inv
    return jnp.cos(ang), jnp.sin(ang)


def apply_rope(x, cos, sin):
    xf = x.astype(jnp.float32)
    x1, x2 = jnp.split(xf, 2, axis=-1)
    return jnp.concatenate([x1 * cos - x2 * sin, x2 * cos + x1 * sin], axis=-1).astype(x.dtype)


def mla_attention(c_q, c_kv, k_rope, cos, sin, q_norm_g, w_uq, kv_norm_g, w_ukv):
    B, S, _ = c_q.shape
    q = (rms_norm(c_q, q_norm_g) @ w_uq).reshape(B, S, MLA_HEADS, MLA_QK)
    q_nope, q_pe = q[..., :MLA_NOPE], q[..., MLA_NOPE:]
    q_pe = apply_rope(q_pe, cos[:, :, None, :], sin[:, :, None, :])
    q = jnp.concatenate([q_nope, q_pe], axis=-1)
    kv = (rms_norm(c_kv, kv_norm_g) @ w_ukv).reshape(B, S, MLA_HEADS, MLA_NOPE + MLA_V)
    k_nope, v = kv[..., :MLA_NOPE], kv[..., MLA_NOPE:]
    k_pe = apply_rope(k_rope, cos, sin)[:, :, None, :]
    k = jnp.concatenate([k_nope, jnp.broadcast_to(k_pe, (B, S, MLA_HEADS, MLA_ROPE))], axis=-1)
    scale = MLA_QK ** -0.5
    nb = S // BLOCK
    qb = q.reshape(B, nb, BLOCK, MLA_HEADS, MLA_QK).transpose(1, 0, 2, 3, 4)
    k_idx = jnp.arange(S)

    def one_block(args):
        qi, i = args
        s = jnp.einsum('bqhd,bkhd->bhqk', qi, k, preferred_element_type=jnp.float32) * scale
        q_idx = i * BLOCK + jnp.arange(BLOCK)
        causal = k_idx[None, :] <= q_idx[:, None]
        s = jnp.where(causal[None, None], s, jnp.finfo(jnp.float32).min)
        p = jax.nn.softmax(s, axis=-1).astype(v.dtype)
        return jnp.einsum('bhqk,bkhd->bqhd', p, v)

    o = lax.map(one_block, (qb, jnp.arange(nb)))
    return o.transpose(1, 0, 2, 3, 4).reshape(B, S, MIX_A)


def swa_attention(q, k, v, positions, sinks):
    B, S, _ = q.shape
    nb = S // BLOCK
    q = q.reshape(B, nb, BLOCK, SWA_KV_HEADS, SWA_GROUP, SWA_HEAD_DIM)
    k = k.reshape(B, S, SWA_KV_HEADS, SWA_HEAD_DIM)
    v = v.reshape(B, S, SWA_KV_HEADS, SWA_HEAD_DIM)

    def band(t):
        pad = [(0, 0), (BLOCK, 0)] + [(0, 0)] * (t.ndim - 2)
        tb = jnp.pad(t, pad).reshape((B, nb + 1, BLOCK) + t.shape[2:])
        return jnp.concatenate([tb[:, :-1], tb[:, 1:]], axis=2)

    k_band, v_band = band(k), band(v)
    k_pos = band(positions)
    q_pos = positions.reshape(B, nb, BLOCK)
    q_idx = jnp.arange(S).reshape(nb, BLOCK)
    k_idx = jnp.arange(-BLOCK, S).reshape(nb + 1, BLOCK)
    k_idx = jnp.concatenate([k_idx[:-1], k_idx[1:]], axis=1)
    delta = q_idx[:, :, None] - k_idx[:, None, :]
    valid = (delta >= 0) & (delta < SWA_WINDOW) & (k_idx[:, None, :] >= 0)
    dist = jnp.abs(q_pos[..., :, None] - k_pos[..., None, :]).astype(jnp.float32)
    slopes = jnp.exp2(-8.0 * jnp.arange(1, SWA_Q_HEADS + 1, dtype=jnp.float32) / SWA_Q_HEADS)
    slopes = slopes.reshape(SWA_KV_HEADS, SWA_GROUP)
    scale = SWA_HEAD_DIM ** -0.5
    s = jnp.einsum('bnqkgd,bnskd->bnkgqs', q, k_band, preferred_element_type=jnp.float32) * scale
    s = s - slopes[None, None, :, :, None, None] * dist[:, :, None, None]
    s = jnp.where(valid[None, :, None, None], s, jnp.finfo(jnp.float32).min)
    sink = sinks.astype(jnp.float32).reshape(SWA_KV_HEADS, SWA_GROUP)[None, None, :, :, None, None]
    m = jnp.maximum(jnp.max(s, axis=-1, keepdims=True), sink)
    e = jnp.exp(s - m)
    p = e / (jnp.sum(e, axis=-1, keepdims=True) + jnp.exp(sink - m))
    o = jnp.einsum('bnkgqs,bnskd->bnqkgd', p.astype(v.dtype), v_band)
    return o.reshape(B, S, MIX_B)


def setup_inputs(seed: int = 0) -> dict:
    key = jax.random.key(seed)
    ks = jax.random.split(key, 20)
    f32 = jnp.float32

    def w(k, shape, fan_in):
        return jax.random.normal(k, shape, f32) * (fan_in ** -0.5)

    def gain(k, n):
        return 1.0 + 0.02 * jax.random.normal(k, (DEPTH, n), f32)

    return {
        "x": jax.random.normal(ks[0], (BATCH, SEQ, D_MODEL), f32),
        "positions": jnp.broadcast_to(jnp.arange(SEQ, dtype=jnp.int32), (BATCH, SEQ)),
        "attn_pre_g": gain(ks[1], D_MODEL),
        "w_in": w(ks[2], (DEPTH, D_MODEL, IN_WIDTH), D_MODEL),
        "q_norm_g": gain(ks[3], MLA_Q_RANK),
        "w_uq": w(ks[4], (DEPTH, MLA_Q_RANK, MLA_HEADS * MLA_QK), MLA_Q_RANK),
        "kv_norm_g": gain(ks[5], MLA_KV_RANK),
        "w_ukv": w(ks[6], (DEPTH, MLA_KV_RANK, MLA_HEADS * (MLA_NOPE + MLA_V)), MLA_KV_RANK),
        "swa_sinks": jax.random.normal(ks[7], (DEPTH, SWA_Q_HEADS), f32),
        "grp_a_g": gain(ks[8], MIX_A),
        "grp_b_g": gain(ks[9], MIX_B),
        "w_o": w(ks[10], (DEPTH, MIX_WIDTH, D_MODEL), MIX_WIDTH),
        "attn_post_g": gain(ks[11], D_MODEL),
        "ffn_pre_g": gain(ks[12], D_MODEL),
        "w_gate": w(ks[13], (DEPTH, D_MODEL, D_FF), D_MODEL),
        "w_up": w(ks[14], (DEPTH, D_MODEL, D_FF), D_MODEL),
        "w_down": w(ks[15], (DEPTH, D_FF, D_MODEL), D_FF),
        "ffn_post_g": gain(ks[16], D_MODEL),
    }


def reference(x, positions, attn_pre_g, w_in, q_norm_g, w_uq, kv_norm_g, w_ukv, swa_sinks,
              grp_a_g, grp_b_g, w_o, attn_post_g, ffn_pre_g, w_gate, w_up, w_down, ffn_post_g):
    cos, sin = rope_tables(positions)
    h = x
    for l in range(DEPTH):
        a = rms_norm(h, attn_pre_g[l])
        proj = a @ w_in[l]
        c_q, c_kv, k_rope, q_s, k_s, v_s = jnp.split(proj, SPLIT_IDX, axis=-1)
        o_a = mla_attention(c_q, c_kv, k_rope, cos, sin, q_norm_g[l], w_uq[l], kv_norm_g[l], w_ukv[l])
        o_b = swa_attention(q_s, k_s, v_s, positions, swa_sinks[l])
        mix = jnp.concatenate([rms_norm(o_a, grp_a_g[l]), rms_norm(o_b, grp_b_g[l])], axis=-1)
        h = h + rms_norm(mix @ w_o[l], attn_post_g[l])
        f = rms_norm(h, ffn_pre_g[l])
        f = (jax.nn.silu(f @ w_gate[l]) * (f @ w_up[l])) @ w_down[l]
        h = h + rms_norm(f, ffn_post_g[l])
    return h
```

```python
import functools

import jax
import jax.numpy as jnp
from jax import lax
from jax.experimental import pallas as pl
from jax.experimental.pallas import tpu as pltpu

D_MODEL = 4096
MLA_HEADS = 16
MLA_Q_RANK = 1024
MLA_KV_RANK = 512
MLA_NOPE = 128
MLA_ROPE = 64
MLA_V = 128
MLA_QK = MLA_NOPE + MLA_ROPE
ROPE_THETA = 10000.0
SWA_Q_HEADS = 32
SWA_KV_HEADS = 8
SWA_HEAD_DIM = 64
SWA_GROUP = SWA_Q_HEADS // SWA_KV_HEADS
SWA_WINDOW = 128
BLOCK = 128
MIX_A = MLA_HEADS * MLA_V
MIX_B = SWA_Q_HEADS * SWA_HEAD_DIM
D_FF = 11008
EPS = 1e-6

LANES = 128
QK_PAD = 256
KV_W = SWA_KV_HEADS * SWA_HEAD_DIM
COL_CQ = 0
COL_CKV = MLA_Q_RANK
COL_QS = COL_CKV + MLA_KV_RANK
COL_KS = COL_QS + MIX_B
COL_VS = COL_KS + KV_W
PROJ_W = COL_VS + KV_W
NEG = float(jnp.finfo(jnp.float32).min)
VMEM_LIMIT = 56 * 1024 * 1024
DOWN_CHUNK = 1024
ROW_CHUNK = 64

F32 = jnp.float32
BF16 = jnp.bfloat16


def _rms(xf, g):
    return xf * lax.rsqrt(jnp.mean(xf * xf, axis=-1, keepdims=True) + EPS) * g


def _for_row_chunks(n_rows, fn, chunk=ROW_CHUNK):
    def body(r, carry):
        fn(pl.ds(pl.multiple_of(r * chunk, chunk), chunk))
        return carry
    lax.fori_loop(0, n_rows // chunk, body, 0)


def _params(sem):
    return pltpu.CompilerParams(dimension_semantics=sem, vmem_limit_bytes=VMEM_LIMIT)


def _inproj_kernel(x_ref, g_ref, w_ref, wr_ref, pos_ref, inv_ref, cmask_ref, sgn_ref,
                   proj_ref, kpe_ref, cos_ref, sin_ref, a_ref):
    @pl.when(pl.program_id(1) == 0)
    def _():
        def norm_rows(rows):
            a_ref[rows, :] = _rms(x_ref[rows, :], g_ref[...]).astype(BF16)
        _for_row_chunks(x_ref.shape[0], norm_rows)
        ang = pos_ref[...].astype(F32) * inv_ref[...]
        c = jnp.cos(ang) * cmask_ref[...]
        s = jnp.sin(ang) * sgn_ref[...]
        cos_ref[...] = c
        sin_ref[...] = s
        kr = jnp.dot(a_ref[...], wr_ref[...], preferred_element_type=F32)
        kpe_ref[...] = (kr * c + pltpu.roll(kr, 64, 1) * s).astype(BF16)

    proj_ref[...] = jnp.dot(a_ref[...], w_ref[...], preferred_element_type=F32).astype(BF16)


def _in_proj(x2, g, w_main, w_rope, pos_col, inv, cmask, sgn, tm=512, tn=768):
    T = x2.shape[0]
    grid = (T // tm, PROJ_W // tn)
    return pl.pallas_call(
        _inproj_kernel,
        grid=grid,
        in_specs=[
            pl.BlockSpec((tm, D_MODEL), lambda i, j: (i, 0)),
            pl.BlockSpec((1, D_MODEL), lambda i, j: (0, 0)),
            pl.BlockSpec((D_MODEL, tn), lambda i, j: (0, j)),
            pl.BlockSpec((D_MODEL, LANES), lambda i, j: (0, 0)),
            pl.BlockSpec((tm, 1), lambda i, j: (i, 0)),
            pl.BlockSpec((1, LANES), lambda i, j: (0, 0)),
            pl.BlockSpec((1, LANES), lambda i, j: (0, 0)),
            pl.BlockSpec((1, LANES), lambda i, j: (0, 0)),
        ],
        out_specs=[
            pl.BlockSpec((tm, tn), lambda i, j: (i, j)),
            pl.BlockSpec((tm, LANES), lambda i, j: (i, 0)),
            pl.BlockSpec((tm, LANES), lambda i, j: (i, 0)),
            pl.BlockSpec((tm, LANES), lambda i, j: (i, 0)),
        ],
        out_shape=[
            jax.ShapeDtypeStruct((T, PROJ_W), BF16),
            jax.ShapeDtypeStruct((T, LANES), BF16),
            jax.ShapeDtypeStruct((T, LANES), F32),
            jax.ShapeDtypeStruct((T, LANES), F32),
        ],
        scratch_shapes=[pltpu.VMEM((tm, D_MODEL), BF16)],
        compiler_params=_params(("parallel", "arbitrary")),
        name="in_proj",
    )(x2, g, w_main, w_rope, pos_col, inv, cmask, sgn)


def _qup_kernel(cq_ref, g_ref, w_ref, cos_ref, sin_ref, q_ref, cqn_ref, *, heads_per_step):
    @pl.when(pl.program_id(1) == 0)
    def _():
        cqn_ref[...] = _rms(cq_ref[...].astype(F32), g_ref[...]).astype(BF16)

    y = jnp.dot(cqn_ref[...], w_ref[...], preferred_element_type=F32)
    scale = MLA_QK ** -0.5
    c = cos_ref[...] * scale
    s = sin_ref[...] * scale
    for h in range(heads_per_step):
        lo = h * QK_PAD
        q_ref[:, lo:lo + MLA_NOPE] = (y[:, lo:lo + MLA_NOPE] * scale).astype(BF16)
        pe = y[:, lo + MLA_NOPE:lo + QK_PAD]
        q_ref[:, lo + MLA_NOPE:lo + QK_PAD] = (pe * c + pltpu.roll(pe, 64, 1) * s).astype(BF16)


def _q_up(proj, g, w_uq_r, cos_t, sin_t, tm=512, tn=1024):
    T = proj.shape[0]
    n_out = MLA_HEADS * QK_PAD
    return pl.pallas_call(
        functools.partial(_qup_kernel, heads_per_step=tn // QK_PAD),
        grid=(T // tm, n_out // tn),
        in_specs=[
            pl.BlockSpec((tm, MLA_Q_RANK), lambda i, j: (i, COL_CQ // MLA_Q_RANK)),
            pl.BlockSpec((1, MLA_Q_RANK), lambda i, j: (0, 0)),
            pl.BlockSpec((MLA_Q_RANK, tn), lambda i, j: (0, j)),
            pl.BlockSpec((tm, LANES), lambda i, j: (i, 0)),
            pl.BlockSpec((tm, LANES), lambda i, j: (i, 0)),
        ],
        out_specs=pl.BlockSpec((tm, tn), lambda i, j: (i, j)),
        out_shape=jax.ShapeDtypeStruct((T, n_out), BF16),
        scratch_shapes=[pltpu.VMEM((tm, MLA_Q_RANK), BF16)],
        compiler_params=_params(("parallel", "arbitrary")),
        name="q_up",
    )(proj, g, w_uq_r, cos_t, sin_t)


def _kvup_kernel(ckv_ref, g_ref, w_ref, kv_ref, cn_ref):
    @pl.when(pl.program_id(1) == 0)
    def _():
        cn_ref[...] = _rms(ckv_ref[...].astype(F32), g_ref[...]).astype(BF16)

    kv_ref[...] = jnp.dot(cn_ref[...], w_ref[...], preferred_element_type=F32).astype(BF16)


def _kv_up(proj, g, w_ukv_r, tm=512, tn=1024):
    T = proj.shape[0]
    n_out = MLA_HEADS * (MLA_NOPE + MLA_V)
    return pl.pallas_call(
        _kvup_kernel,
        grid=(T // tm, n_out // tn),
        in_specs=[
            pl.BlockSpec((tm, MLA_KV_RANK), lambda i, j: (i, COL_CKV // MLA_KV_RANK)),
            pl.BlockSpec((1, MLA_KV_RANK), lambda i, j: (0, 0)),
            pl.BlockSpec((MLA_KV_RANK, tn), lambda i, j: (0, j)),
        ],
        out_specs=pl.BlockSpec((tm, tn), lambda i, j: (i, j)),
        out_shape=jax.ShapeDtypeStruct((T, n_out), BF16),
        scratch_shapes=[pltpu.VMEM((tm, MLA_KV_RANK), BF16)],
        compiler_params=_params(("parallel", "arbitrary")),
        name="kv_up",
    )(proj, g, w_ukv_r)


def _mla_kernel(q_ref, kn_ref, kpe_ref, v_ref, o_ref, m_ref, l_ref, acc_ref, *, tq):
    qi = pl.program_id(2)
    q = q_ref[...]
    m_ref[...] = jnp.full(m_ref.shape, NEG, F32)
    l_ref[...] = jnp.zeros(l_ref.shape, F32)
    acc_ref[...] = jnp.zeros(acc_ref.shape, F32)

    def tile(j, masked):
        ks = pl.multiple_of(j * tq, tq)
        k = jnp.concatenate([kn_ref[pl.ds(ks, tq), :], kpe_ref[pl.ds(ks, tq), :]], axis=1)
        s = lax.dot_general(q, k, (((1,), (1,)), ((), ())), preferred_element_type=F32)
        if masked:
            row = lax.broadcasted_iota(jnp.int32, s.shape, 0)
            col = lax.broadcasted_iota(jnp.int32, s.shape, 1)
            s = jnp.where(col <= row, s, NEG)
        m_prev = m_ref[...]
        m_new = jnp.maximum(m_prev, jnp.max(s, axis=-1, keepdims=True))
        alpha = jnp.exp(m_prev - m_new)
        p = jnp.exp(s - m_new)
        l_ref[...] = alpha * l_ref[...] + jnp.sum(p, axis=-1, keepdims=True)
        acc_ref[...] = alpha * acc_ref[...] + jnp.dot(
            p.astype(BF16), v_ref[pl.ds(ks, tq), :], preferred_element_type=F32)
        m_ref[...] = m_new

    def body(j, carry):
        tile(j, False)
        return carry

    lax.fori_loop(0, qi, body, 0)
    tile(qi, True)
    o_ref[...] = (acc_ref[...] / l_ref[...]).astype(BF16)


def _mla_attention(q, kv, kpe, B, S, tq=512):
    T = B * S
    nq = S // tq
    return pl.pallas_call(
        functools.partial(_mla_kernel, tq=tq),
        grid=(B, MLA_HEADS, nq),
        in_specs=[
            pl.BlockSpec((tq, QK_PAD), lambda b, h, i: (b * nq + i, h)),
            pl.BlockSpec((S, MLA_NOPE), lambda b, h, i: (b, h)),
            pl.BlockSpec((S, LANES), lambda b, h, i: (b, 0)),
            pl.BlockSpec((S, MLA_V), lambda b, h, i: (b, MLA_HEADS + h)),
        ],
        out_specs=pl.BlockSpec((tq, MLA_V), lambda b, h, i: (b * nq + i, h)),
        out_shape=jax.ShapeDtypeStruct((T, MIX_A), BF16),
        scratch_shapes=[
            pltpu.VMEM((tq, 1), F32),
            pltpu.VMEM((tq, 1), F32),
            pltpu.VMEM((tq, MLA_V), F32),
        ],
        compiler_params=_params(("parallel", "parallel", "arbitrary")),
        name="mla_attn",
    )(q, kv, kpe, kv)


def _swa_kernel(slope_ref, sink_ref, q0_ref, q1_ref, q2_ref, q3_ref, k_ref, kp_ref, v_ref, vp_ref,
                pc_ref, pr_ref, prp_ref, o_ref, *, tq):
    n = pl.program_id(1)
    pair = pl.program_id(2)
    q_refs = (q0_ref, q1_ref, q2_ref, q3_ref)
    lane = lax.broadcasted_iota(jnp.int32, (2 * BLOCK, LANES), 1)
    lo = lane < SWA_HEAD_DIM
    k_all = jnp.concatenate([kp_ref[...], k_ref[...]], axis=0)
    v_all = jnp.concatenate([vp_ref[...], v_ref[...]], axis=0)
    pos_k = jnp.concatenate([prp_ref[0], pr_ref[0]], axis=1)
    i_idx = lax.broadcasted_iota(jnp.int32, (BLOCK, 2 * BLOCK), 0)
    c_idx = lax.broadcasted_iota(jnp.int32, (BLOCK, 2 * BLOCK), 1)
    in_window = (c_idx > i_idx) & (c_idx <= i_idx + SWA_WINDOW)
    zero = jnp.zeros((), BF16)
    scale = SWA_HEAD_DIM ** -0.5

    for t in range(tq // BLOCK):
        r0 = t * BLOCK
        band_k = k_all[r0:r0 + 2 * BLOCK]
        band_v = v_all[r0:r0 + 2 * BLOCK]
        k_bd = jnp.concatenate([jnp.where(lo, band_k, zero), jnp.where(lo, zero, band_k)], axis=0)
        v_bd = jnp.concatenate([jnp.where(lo, band_v, zero), jnp.where(lo, zero, band_v)], axis=0)
        q_st = jnp.concatenate([qr[r0:r0 + BLOCK, :] for qr in q_refs], axis=0)
        s = lax.dot_general(q_st, k_bd, (((1,), (1,)), ((), ())), preferred_element_type=F32)
        dist = jnp.abs(pc_ref[r0:r0 + BLOCK, :] - pos_k[:, r0:r0 + 2 * BLOCK]).astype(F32)
        valid = in_window
        if t == 0:
            valid = valid & ((c_idx >= BLOCK) | (n > 0))
        rows = []
        for g in range(SWA_GROUP):
            halves = []
            for c in range(2):
                head = (2 * pair + c) * SWA_GROUP + g
                slope = slope_ref[head]
                sink = sink_ref[head]
                sg = s[g * BLOCK:(g + 1) * BLOCK, c * 2 * BLOCK:(c + 1) * 2 * BLOCK] * scale
                sg = sg - slope * dist
                sg = jnp.where(valid, sg, NEG)
                m = jnp.maximum(jnp.max(sg, axis=-1, keepdims=True), sink)
                e = jnp.exp(sg - m)
                den = jnp.sum(e, axis=-1, keepdims=True) + jnp.exp(sink - m)
                halves.append((e / den).astype(BF16))
            rows.append(jnp.concatenate(halves, axis=1))
        p = jnp.concatenate(rows, axis=0)
        out = jnp.dot(p, v_bd, preferred_element_type=F32)
        for g in range(SWA_GROUP):
            o_ref[g, r0:r0 + BLOCK, :] = out[g * BLOCK:(g + 1) * BLOCK].astype(BF16)


def _swa_attention(proj, pos_col, pos_row, slopes, sinks, B, S, tq=512):
    T = B * S
    nt = S // tq
    sub = tq // BLOCK
    npairs = SWA_KV_HEADS // 2
    qs0 = COL_QS // LANES
    ks0 = COL_KS // LANES
    vs0 = COL_VS // LANES
    pairs_per_group = KV_W // LANES

    def q_spec(g):
        return pl.BlockSpec((tq, LANES), lambda b, n, j, g=g: (b * nt + n, qs0 + pairs_per_group * g + j))

    def prev_row(b, n):
        return jnp.maximum(b * (S // BLOCK) + sub * n - 1, 0)

    smem = pl.BlockSpec(memory_space=pltpu.SMEM)
    return pl.pallas_call(
        functools.partial(_swa_kernel, tq=tq),
        grid=(B, nt, npairs),
        in_specs=[
            smem, smem,
            q_spec(0), q_spec(1), q_spec(2), q_spec(3),
            pl.BlockSpec((tq, LANES), lambda b, n, j: (b * nt + n, ks0 + j)),
            pl.BlockSpec((BLOCK, LANES), lambda b, n, j: (prev_row(b, n), ks0 + j)),
            pl.BlockSpec((tq, LANES), lambda b, n, j: (b * nt + n, vs0 + j)),
            pl.BlockSpec((BLOCK, LANES), lambda b, n, j: (prev_row(b, n), vs0 + j)),
            pl.BlockSpec((tq, 1), lambda b, n, j: (b * nt + n, 0)),
            pl.BlockSpec((1, 1, tq), lambda b, n, j: (b, 0, n)),
            pl.BlockSpec((1, 1, BLOCK), lambda b, n, j: (b, 0, jnp.maximum(sub * n - 1, 0))),
        ],
        out_specs=pl.BlockSpec((SWA_GROUP, tq, LANES), lambda b, n, j: (0, b * nt + n, j)),
        out_shape=jax.ShapeDtypeStruct((SWA_GROUP, T, KV_W), BF16),
        compiler_params=_params(("parallel", "parallel", "arbitrary")),
        name="swa_attn",
    )(slopes, sinks, proj, proj, proj, proj, proj, proj, proj, proj, pos_col, pos_row, pos_row)


def _oproj_kernel(oa_ref, ob_ref, ga_ref, gb_ref, w_ref, x_ref, gp_ref, h_ref, mix_ref, y_ref, *, nj, tn):
    j = pl.program_id(1)

    @pl.when(j == 0)
    def _():
        def norm_rows(rows):
            mix_ref[rows, :MIX_A] = _rms(oa_ref[rows, :].astype(F32), ga_ref[...]).astype(BF16)
            ob = jnp.concatenate([ob_ref[g, rows, :] for g in range(SWA_GROUP)], axis=1).astype(F32)
            mix_ref[rows, MIX_A:] = _rms(ob, gb_ref[...]).astype(BF16)
        _for_row_chunks(mix_ref.shape[0], norm_rows)

    y_ref[j] = jnp.dot(mix_ref[...], w_ref[...], preferred_element_type=F32)

    @pl.when(j == nj - 1)
    def _():
        def finish_rows(rows):
            ssq = jnp.zeros((ROW_CHUNK, 1), F32)
            for jj in range(nj):
                y = y_ref[jj, rows, :]
                ssq = ssq + jnp.sum(y * y, axis=-1, keepdims=True)
            r = lax.rsqrt(ssq * (1.0 / D_MODEL) + EPS)
            for jj in range(nj):
                cs = slice(jj * tn, (jj + 1) * tn)
                h_ref[rows, cs] = x_ref[rows, cs] + y_ref[jj, rows, :] * r * gp_ref[:, cs]
        _for_row_chunks(h_ref.shape[0], finish_rows)


def _out_proj(o_a, o_b, ga, gb, w_o_r, x2, gpost, tm=256, tn=512):
    T = x2.shape[0]
    nj = D_MODEL // tn
    return pl.pallas_call(
        functools.partial(_oproj_kernel, nj=nj, tn=tn),
        grid=(T // tm, nj),
        in_specs=[
            pl.BlockSpec((tm, MIX_A), lambda i, j: (i, 0)),
            pl.BlockSpec((SWA_GROUP, tm, KV_W), lambda i, j: (0, i, 0)),
            pl.BlockSpec((1, MIX_A), lambda i, j: (0, 0)),
            pl.BlockSpec((1, MIX_B), lambda i, j: (0, 0)),
            pl.BlockSpec((MIX_A + MIX_B, tn), lambda i, j: (0, j)),
            pl.BlockSpec((tm, D_MODEL), lambda i, j: (i, 0)),
            pl.BlockSpec((1, D_MODEL), lambda i, j: (0, 0)),
        ],
        out_specs=pl.BlockSpec((tm, D_MODEL), lambda i, j: (i, 0)),
        out_shape=jax.ShapeDtypeStruct((T, D_MODEL), F32),
        scratch_shapes=[
            pltpu.VMEM((tm, MIX_A + MIX_B), BF16),
            pltpu.VMEM((nj, tm, tn), F32),
        ],
        compiler_params=_params(("parallel", "arbitrary")),
        name="out_proj",
    )(o_a, o_b, ga, gb, w_o_r, x2, gpost)


def _ffn_kernel(h_ref, gpre_ref, wg_ref, wu_ref, wd_ref, gpost_ref, o_ref, f_ref, *, nj):
    j = pl.program_id(1)

    @pl.when(j == 0)
    def _():
        def norm_rows(rows):
            f_ref[rows, :] = _rms(h_ref[rows, :], gpre_ref[...]).astype(BF16)
            o_ref[rows, :] = jnp.zeros((ROW_CHUNK, D_MODEL), F32)
        _for_row_chunks(f_ref.shape[0], norm_rows)

    f = f_ref[...]
    gate = jnp.dot(f, wg_ref[...], preferred_element_type=F32)
    up = jnp.dot(f, wu_ref[...], preferred_element_type=F32)
    act = (gate * jax.nn.sigmoid(gate) * up).astype(BF16)
    for c in range(D_MODEL // DOWN_CHUNK):
        cs = slice(c * DOWN_CHUNK, (c + 1) * DOWN_CHUNK)
        o_ref[:, cs] += jnp.dot(act, wd_ref[:, cs], preferred_element_type=F32)

    @pl.when(j == nj - 1)
    def _():
        def finish_rows(rows):
            o_ref[rows, :] = h_ref[rows, :] + _rms(o_ref[rows, :], gpost_ref[...])
        _for_row_chunks(o_ref.shape[0], finish_rows)


def _ffn(h, gpre, wg, wu, wd, gpost, tm=512, tf=256):
    T = h.shape[0]
    nj = D_FF // tf
    return pl.pallas_call(
        functools.partial(_ffn_kernel, nj=nj),
        grid=(T // tm, nj),
        in_specs=[
            pl.BlockSpec((tm, D_MODEL), lambda i, j: (i, 0)),
            pl.BlockSpec((1, D_MODEL), lambda i, j: (0, 0)),
            pl.BlockSpec((D_MODEL, tf), lambda i, j: (0, j)),
            pl.BlockSpec((D_MODEL, tf), lambda i, j: (0, j)),
            pl.BlockSpec((tf, D_MODEL), lambda i, j: (j, 0)),
            pl.BlockSpec((1, D_MODEL), lambda i, j: (0, 0)),
        ],
        out_specs=pl.BlockSpec((tm, D_MODEL), lambda i, j: (i, 0)),
        out_shape=jax.ShapeDtypeStruct((T, D_MODEL), F32),
        scratch_shapes=[pltpu.VMEM((tm, D_MODEL), BF16)],
        compiler_params=_params(("parallel", "arbitrary")),
        name="ffn",
    )(h, gpre, wg, wu, wd, gpost)


def _swap_halves(w):
    half = w.shape[-1] // 2
    return jnp.concatenate([w[..., half:], w[..., :half]], axis=-1)


def _prep_w_in(w_in):
    c0 = MLA_Q_RANK + MLA_KV_RANK
    c1 = c0 + MLA_ROPE
    c2 = c1 + MIX_B
    kr = w_in[:, c0:c1]
    qs = w_in[:, c1:c2].reshape(D_MODEL, SWA_KV_HEADS, SWA_GROUP, SWA_HEAD_DIM)
    qs = qs.transpose(0, 2, 1, 3).reshape(D_MODEL, MIX_B)
    w_main = jnp.concatenate([w_in[:, :c0], qs, w_in[:, c2:]], axis=1).astype(BF16)
    w_rope = jnp.concatenate([kr, _swap_halves(kr)], axis=1).astype(BF16)
    return w_main, w_rope


def _prep_w_uq(w_uq):
    w = w_uq.reshape(MLA_Q_RANK, MLA_HEADS, MLA_QK)
    pe = w[..., MLA_NOPE:]
    w = jnp.concatenate([w[..., :MLA_NOPE], pe, _swap_halves(pe)], axis=-1)
    return w.reshape(MLA_Q_RANK, MLA_HEADS * QK_PAD).astype(BF16)


def _prep_w_ukv(w_ukv):
    w = w_ukv.reshape(MLA_KV_RANK, MLA_HEADS, MLA_NOPE + MLA_V)
    kn = w[..., :MLA_NOPE].reshape(MLA_KV_RANK, MLA_HEADS * MLA_NOPE)
    v = w[..., MLA_NOPE:].reshape(MLA_KV_RANK, MLA_HEADS * MLA_V)
    return jnp.concatenate([kn, v], axis=1).astype(BF16)


def _swa_perm_rows(a):
    rest = a.shape[1:]
    a = a.reshape((SWA_KV_HEADS, SWA_GROUP, SWA_HEAD_DIM) + rest)
    return jnp.swapaxes(a, 0, 1).reshape((MIX_B,) + rest)


def kernel(x, positions, attn_pre_g, w_in, q_norm_g, w_uq, kv_norm_g, w_ukv, swa_sinks, grp_a_g, grp_b_g,
           w_o, attn_post_g, ffn_pre_g, w_gate, w_up, w_down, ffn_post_g):
    B, S, _ = x.shape
    T = B * S
    depth = w_in.shape[0]
    inv = 1.0 / (ROPE_THETA ** (jnp.arange(0, MLA_ROPE, 2, dtype=F32) / MLA_ROPE))
    zeros64 = jnp.zeros((MLA_ROPE,), F32)
    inv_l = jnp.concatenate([inv, inv, zeros64]).reshape(1, LANES)
    cmask = jnp.concatenate([jnp.ones((MLA_ROPE,), F32), zeros64]).reshape(1, LANES)
    half = MLA_ROPE // 2
    sgn = jnp.concatenate([-jnp.ones((half,), F32), jnp.ones((half,), F32), zeros64]).reshape(1, LANES)
    slopes = jnp.exp2(-8.0 * jnp.arange(1, SWA_Q_HEADS + 1, dtype=F32) / SWA_Q_HEADS)
    pos_col = positions.reshape(T, 1)
    pos_row = positions.reshape(B, 1, S)

    h = x.reshape(T, D_MODEL)
    for l in range(depth):
        w_main, w_rope = _prep_w_in(w_in[l])
        w_o_r = jnp.concatenate([w_o[l][:MIX_A], _swa_perm_rows(w_o[l][MIX_A:])], axis=0).astype(BF16)
        gb = _swa_perm_rows(grp_b_g[l]).reshape(1, MIX_B)

        proj, kpe, cos_t, sin_t = _in_proj(h, attn_pre_g[l].reshape(1, D_MODEL), w_main, w_rope,
                                           pos_col, inv_l, cmask, sgn)
        q = _q_up(proj, q_norm_g[l].reshape(1, MLA_Q_RANK), _prep_w_uq(w_uq[l]), cos_t, sin_t)
        kv = _kv_up(proj, kv_norm_g[l].reshape(1, MLA_KV_RANK), _prep_w_ukv(w_ukv[l]))
        o_a = _mla_attention(q, kv, kpe, B, S)
        o_b = _swa_attention(proj, pos_col, pos_row, slopes, swa_sinks[l].astype(F32), B, S)
        h = _out_proj(o_a, o_b, grp_a_g[l].reshape(1, MIX_A), gb, w_o_r, h,
                      attn_post_g[l].reshape(1, D_MODEL))
        h = _ffn(h, ffn_pre_g[l].reshape(1, D_MODEL), w_gate[l].astype(BF16), w_up[l].astype(BF16),
                 w_down[l].astype(BF16), ffn_post_g[l].reshape(1, D_MODEL))
    return h.reshape(B, S, D_MODEL)
```

```python
import functools

import jax
import jax.numpy as jnp
from jax import lax
from jax.experimental import pallas as pl
from jax.experimental.pallas import tpu as pltpu

D_MODEL = 4096
MLA_HEADS = 16
MLA_Q_RANK = 1024
MLA_KV_RANK = 512
MLA_NOPE = 128
MLA_ROPE = 64
MLA_V = 128
MLA_QK = MLA_NOPE + MLA_ROPE
ROPE_THETA = 10000.0
SWA_Q_HEADS = 32
SWA_KV_HEADS = 8
SWA_HEAD_DIM = 64
SWA_GROUP = SWA_Q_HEADS // SWA_KV_HEADS
SWA_WINDOW = 128
BLOCK = 128
MIX_A = MLA_HEADS * MLA_V
MIX_B = SWA_Q_HEADS * SWA_HEAD_DIM
D_FF = 11008
EPS = 1e-6

LANES = 128
QK_PAD = 256
KV_W = SWA_KV_HEADS * SWA_HEAD_DIM
COL_CQ = 0
COL_CKV = MLA_Q_RANK
COL_QS = COL_CKV + MLA_KV_RANK
COL_KS = COL_QS + MIX_B
COL_VS = COL_KS + KV_W
PROJ_W = COL_VS + KV_W
NEG = float(jnp.finfo(jnp.float32).min)
VMEM_LIMIT = 56 * 1024 * 1024
DOWN_CHUNK = 1024
ROW_CHUNK = 64
LOG2_E = 1.4426950408889634

F32 = jnp.float32
BF16 = jnp.bfloat16


def _rms(xf, g):
    return xf * lax.rsqrt(jnp.mean(xf * xf, axis=-1, keepdims=True) + EPS) * g


def _for_row_chunks(n_rows, fn, chunk=ROW_CHUNK):
    def body(r, carry):
        fn(pl.ds(pl.multiple_of(r * chunk, chunk), chunk))
        return carry
    lax.fori_loop(0, n_rows // chunk, body, 0)


def _params(sem):
    return pltpu.CompilerParams(dimension_semantics=sem, vmem_limit_bytes=VMEM_LIMIT)


def _inproj_kernel(x_ref, g_ref, w_ref, wr_ref, pos_ref, inv_ref, cmask_ref, sgn_ref,
                   proj_ref, kpe_ref, cos_ref, sin_ref, a_ref):
    @pl.when(pl.program_id(1) == 0)
    def _():
        def norm_rows(rows):
            a_ref[rows, :] = _rms(x_ref[rows, :], g_ref[...]).astype(BF16)
        _for_row_chunks(x_ref.shape[0], norm_rows)
        ang = pos_ref[...].astype(F32) * inv_ref[...]
        c = jnp.cos(ang) * cmask_ref[...]
        s = jnp.sin(ang) * sgn_ref[...]
        cos_ref[...] = c
        sin_ref[...] = s
        kr = jnp.dot(a_ref[...], wr_ref[...], preferred_element_type=F32)
        kpe_ref[...] = (kr * c + pltpu.roll(kr, 64, 1) * s).astype(BF16)

    proj_ref[...] = jnp.dot(a_ref[...], w_ref[...], preferred_element_type=F32).astype(BF16)


def _in_proj(x2, g, w_main, w_rope, pos_col, inv, cmask, sgn, tm=512, tn=768):
    T = x2.shape[0]
    grid = (T // tm, PROJ_W // tn)
    return pl.pallas_call(
        _inproj_kernel,
        grid=grid,
        in_specs=[
            pl.BlockSpec((tm, D_MODEL), lambda i, j: (i, 0)),
            pl.BlockSpec((1, D_MODEL), lambda i, j: (0, 0)),
            pl.BlockSpec((D_MODEL, tn), lambda i, j: (0, j)),
            pl.BlockSpec((D_MODEL, LANES), lambda i, j: (0, 0)),
            pl.BlockSpec((tm, 1), lambda i, j: (i, 0)),
            pl.BlockSpec((1, LANES), lambda i, j: (0, 0)),
            pl.BlockSpec((1, LANES), lambda i, j: (0, 0)),
            pl.BlockSpec((1, LANES), lambda i, j: (0, 0)),
        ],
        out_specs=[
            pl.BlockSpec((tm, tn), lambda i, j: (i, j)),
            pl.BlockSpec((tm, LANES), lambda i, j: (i, 0)),
            pl.BlockSpec((tm, LANES), lambda i, j: (i, 0)),
            pl.BlockSpec((tm, LANES), lambda i, j: (i, 0)),
        ],
        out_shape=[
            jax.ShapeDtypeStruct((T, PROJ_W), BF16),
            jax.ShapeDtypeStruct((T, LANES), BF16),
            jax.ShapeDtypeStruct((T, LANES), F32),
            jax.ShapeDtypeStruct((T, LANES), F32),
        ],
        scratch_shapes=[pltpu.VMEM((tm, D_MODEL), BF16)],
        compiler_params=_params(("parallel", "arbitrary")),
        name="in_proj",
    )(x2, g, w_main, w_rope, pos_col, inv, cmask, sgn)


def _qup_kernel(cq_ref, g_ref, w_ref, cos_ref, sin_ref, q_ref, cqn_ref, *, heads_per_step):
    @pl.when(pl.program_id(1) == 0)
    def _():
        cqn_ref[...] = _rms(cq_ref[...].astype(F32), g_ref[...]).astype(BF16)

    y = jnp.dot(cqn_ref[...], w_ref[...], preferred_element_type=F32)
    scale = MLA_QK ** -0.5 * LOG2_E
    c = cos_ref[...] * scale
    s = sin_ref[...] * scale
    for h in range(heads_per_step):
        lo = h * QK_PAD
        q_ref[:, lo:lo + MLA_NOPE] = (y[:, lo:lo + MLA_NOPE] * scale).astype(BF16)
        pe = y[:, lo + MLA_NOPE:lo + QK_PAD]
        q_ref[:, lo + MLA_NOPE:lo + QK_PAD] = (pe * c + pltpu.roll(pe, 64, 1) * s).astype(BF16)


def _q_up(proj, g, w_uq_r, cos_t, sin_t, tm=512, tn=1024):
    T = proj.shape[0]
    n_out = MLA_HEADS * QK_PAD
    return pl.pallas_call(
        functools.partial(_qup_kernel, heads_per_step=tn // QK_PAD),
        grid=(T // tm, n_out // tn),
        in_specs=[
            pl.BlockSpec((tm, MLA_Q_RANK), lambda i, j: (i, COL_CQ // MLA_Q_RANK)),
            pl.BlockSpec((1, MLA_Q_RANK), lambda i, j: (0, 0)),
            pl.BlockSpec((MLA_Q_RANK, tn), lambda i, j: (0, j)),
            pl.BlockSpec((tm, LANES), lambda i, j: (i, 0)),
            pl.BlockSpec((tm, LANES), lambda i, j: (i, 0)),
        ],
        out_specs=pl.BlockSpec((tm, tn), lambda i, j: (i, j)),
        out_shape=jax.ShapeDtypeStruct((T, n_out), BF16),
        scratch_shapes=[pltpu.VMEM((tm, MLA_Q_RANK), BF16)],
        compiler_params=_params(("parallel", "arbitrary")),
        name="q_up",
    )(proj, g, w_uq_r, cos_t, sin_t)


def _kvup_kernel(ckv_ref, g_ref, w_ref, kv_ref, cn_ref):
    @pl.when(pl.program_id(1) == 0)
    def _():
        cn_ref[...] = _rms(ckv_ref[...].astype(F32), g_ref[...]).astype(BF16)

    kv_ref[...] = jnp.dot(cn_ref[...], w_ref[...], preferred_element_type=F32).astype(BF16)


def _kv_up(proj, g, w_ukv_r, tm=512, tn=1024):
    T = proj.shape[0]
    n_out = MLA_HEADS * (MLA_NOPE + MLA_V)
    return pl.pallas_call(
        _kvup_kernel,
        grid=(T // tm, n_out // tn),
        in_specs=[
            pl.BlockSpec((tm, MLA_KV_RANK), lambda i, j: (i, COL_CKV // MLA_KV_RANK)),
            pl.BlockSpec((1, MLA_KV_RANK), lambda i, j: (0, 0)),
            pl.BlockSpec((MLA_KV_RANK, tn), lambda i, j: (0, j)),
        ],
        out_specs=pl.BlockSpec((tm, tn), lambda i, j: (i, j)),
        out_shape=jax.ShapeDtypeStruct((T, n_out), BF16),
        scratch_shapes=[pltpu.VMEM((tm, MLA_KV_RANK), BF16)],
        compiler_params=_params(("parallel", "arbitrary")),
        name="kv_up",
    )(proj, g, w_ukv_r)


def _mla_kernel(q_ref, kn_ref, kpe_ref, v_ref, o_ref, m_ref, l_ref, acc_ref, *, th):
    qi = pl.program_id(2)
    m_ref[...] = jnp.full(m_ref.shape, NEG, F32)
    l_ref[...] = jnp.zeros(l_ref.shape, F32)
    acc_ref[...] = jnp.zeros(acc_ref.shape, F32)
    n_chunks = th // LANES

    def load_kv(j):
        ks = pl.multiple_of(j * th, th)
        k = jnp.concatenate([kn_ref[pl.ds(ks, th), :], kpe_ref[pl.ds(ks, th), :]], axis=1)
        return k, v_ref[pl.ds(ks, th), :]

    def update(half, k, v, masked):
        q = q_ref[half * th:(half + 1) * th, :]
        s = lax.dot_general(q, k, (((1,), (1,)), ((), ())), preferred_element_type=F32)
        if masked:
            row = lax.broadcasted_iota(jnp.int32, s.shape, 0)
            col = lax.broadcasted_iota(jnp.int32, s.shape, 1)
            s = jnp.where(col <= row, s, NEG)
        chunks = [s[:, c * LANES:(c + 1) * LANES] for c in range(n_chunks)]
        mx = chunks[0]
        for c in chunks[1:]:
            mx = jnp.maximum(mx, c)
        m_prev = m_ref[half]
        m_new = jnp.maximum(m_prev, jnp.max(mx, axis=-1, keepdims=True))
        alpha = jnp.exp2(m_prev - m_new)
        ps = [jnp.exp2(c - m_new) for c in chunks]
        psum = ps[0]
        for p in ps[1:]:
            psum = psum + p
        l_ref[half] = alpha * l_ref[half] + psum
        p = jnp.concatenate(ps, axis=1).astype(BF16)
        acc_ref[half] = alpha * acc_ref[half] + jnp.dot(p, v, preferred_element_type=F32)
        m_ref[half] = m_new

    def body(j, carry):
        k, v = load_kv(j)
        update(0, k, v, False)
        update(1, k, v, False)
        return carry

    lax.fori_loop(0, 2 * qi, body, 0)
    k, v = load_kv(2 * qi)
    update(0, k, v, True)
    update(1, k, v, False)
    k, v = load_kv(2 * qi + 1)
    update(1, k, v, True)
    for half in range(2):
        l = jnp.sum(l_ref[half], axis=-1, keepdims=True)
        o_ref[half * th:(half + 1) * th, :] = (acc_ref[half] / l).astype(BF16)


def _mla_attention(q, kv, kpe, B, S, th=512):
    T = B * S
    tq = 2 * th
    nq = S // tq
    return pl.pallas_call(
        functools.partial(_mla_kernel, th=th),
        grid=(B, MLA_HEADS, nq),
        in_specs=[
            pl.BlockSpec((tq, QK_PAD), lambda b, h, i: (b * nq + i, h)),
            pl.BlockSpec((S, MLA_NOPE), lambda b, h, i: (b, h)),
            pl.BlockSpec((S, LANES), lambda b, h, i: (b, 0)),
            pl.BlockSpec((S, MLA_V), lambda b, h, i: (b, MLA_HEADS + h)),
        ],
        out_specs=pl.BlockSpec((tq, MLA_V), lambda b, h, i: (b * nq + i, h)),
        out_shape=jax.ShapeDtypeStruct((T, MIX_A), BF16),
        scratch_shapes=[
            pltpu.VMEM((2, th, LANES), F32),
            pltpu.VMEM((2, th, LANES), F32),
            pltpu.VMEM((2, th, MLA_V), F32),
        ],
        compiler_params=_params(("parallel", "parallel", "arbitrary")),
        name="mla_attn",
    )(q, kv, kpe, kv)


def _swa_kernel(slope_ref, sink_ref, q0_ref, q1_ref, q2_ref, q3_ref, k_ref, kp_ref, v_ref, vp_ref,
                pc_ref, pr_ref, prp_ref, o_ref, *, tq):
    n = pl.program_id(1)
    pair = pl.program_id(2)
    q_refs = (q0_ref, q1_ref, q2_ref, q3_ref)
    lane = lax.broadcasted_iota(jnp.int32, (2 * BLOCK, LANES), 1)
    lo = lane < SWA_HEAD_DIM
    k_all = jnp.concatenate([kp_ref[...], k_ref[...]], axis=0)
    v_all = jnp.concatenate([vp_ref[...], v_ref[...]], axis=0)
    pos_k = jnp.concatenate([prp_ref[0], pr_ref[0]], axis=1)
    i_idx = lax.broadcasted_iota(jnp.int32, (BLOCK, 2 * BLOCK), 0)
    c_idx = lax.broadcasted_iota(jnp.int32, (BLOCK, 2 * BLOCK), 1)
    in_window = (c_idx > i_idx) & (c_idx <= i_idx + SWA_WINDOW)
    zero = jnp.zeros((), BF16)
    scale = SWA_HEAD_DIM ** -0.5

    for t in range(tq // BLOCK):
        r0 = t * BLOCK
        band_k = k_all[r0:r0 + 2 * BLOCK]
        band_v = v_all[r0:r0 + 2 * BLOCK]
        k_bd = jnp.concatenate([jnp.where(lo, band_k, zero), jnp.where(lo, zero, band_k)], axis=0)
        v_bd = jnp.concatenate([jnp.where(lo, band_v, zero), jnp.where(lo, zero, band_v)], axis=0)
        q_st = jnp.concatenate([qr[r0:r0 + BLOCK, :] for qr in q_refs], axis=0)
        s = lax.dot_general(q_st, k_bd, (((1,), (1,)), ((), ())), preferred_element_type=F32)
        dist = jnp.abs(pc_ref[r0:r0 + BLOCK, :] - pos_k[:, r0:r0 + 2 * BLOCK]).astype(F32)
        valid = in_window
        if t == 0:
            valid = valid & ((c_idx >= BLOCK) | (n > 0))
        rows = []
        for g in range(SWA_GROUP):
            halves = []
            for c in range(2):
                head = (2 * pair + c) * SWA_GROUP + g
                slope = slope_ref[head]
                sink = sink_ref[head]
                sg = s[g * BLOCK:(g + 1) * BLOCK, c * 2 * BLOCK:(c + 1) * 2 * BLOCK] * scale
                sg = sg - slope * dist
                sg = jnp.where(valid, sg, NEG)
                m = jnp.maximum(jnp.max(sg, axis=-1, keepdims=True), sink)
                e = jnp.exp(sg - m)
                den = jnp.sum(e, axis=-1, keepdims=True) + jnp.exp(sink - m)
                halves.append((e / den).astype(BF16))
            rows.append(jnp.concatenate(halves, axis=1))
        p = jnp.concatenate(rows, axis=0)
        out = jnp.dot(p, v_bd, preferred_element_type=F32)
        for g in range(SWA_GROUP):
            o_ref[g, r0:r0 + BLOCK, :] = out[g * BLOCK:(g + 1) * BLOCK].astype(BF16)


def _swa_attention(proj, pos_col, pos_row, slopes, sinks, B, S, tq=512):
    T = B * S
    nt = S // tq
    sub = tq // BLOCK
    npairs = SWA_KV_HEADS // 2
    qs0 = COL_QS // LANES
    ks0 = COL_KS // LANES
    vs0 = COL_VS // LANES
    pairs_per_group = KV_W // LANES

    def q_spec(g):
        return pl.BlockSpec((tq, LANES), lambda b, n, j, g=g: (b * nt + n, qs0 + pairs_per_group * g + j))

    def prev_row(b, n):
        return jnp.maximum(b * (S // BLOCK) + sub * n - 1, 0)

    smem = pl.BlockSpec(memory_space=pltpu.SMEM)
    return pl.pallas_call(
        functools.partial(_swa_kernel, tq=tq),
        grid=(B, nt, npairs),
        in_specs=[
            smem, smem,
            q_spec(0), q_spec(1), q_spec(2), q_spec(3),
            pl.BlockSpec((tq, LANES), lambda b, n, j: (b * nt + n, ks0 + j)),
            pl.BlockSpec((BLOCK, LANES), lambda b, n, j: (prev_row(b, n), ks0 + j)),
            pl.BlockSpec((tq, LANES), lambda b, n, j: (b * nt + n, vs0 + j)),
            pl.BlockSpec((BLOCK, LANES), lambda b, n, j: (prev_row(b, n), vs0 + j)),
            pl.BlockSpec((tq, 1), lambda b, n, j: (b * nt + n, 0)),
            pl.BlockSpec((1, 1, tq), lambda b, n, j: (b, 0, n)),
            pl.BlockSpec((1, 1, BLOCK), lambda b, n, j: (b, 0, jnp.maximum(sub * n - 1, 0))),
        ],
        out_specs=pl.BlockSpec((SWA_GROUP, tq, LANES), lambda b, n, j: (0, b * nt + n, j)),
        out_shape=jax.ShapeDtypeStruct((SWA_GROUP, T, KV_W), BF16),
        compiler_params=_params(("parallel", "parallel", "arbitrary")),
        name="swa_attn",
    )(slopes, sinks, proj, proj, proj, proj, proj, proj, proj, proj, pos_col, pos_row, pos_row)


def _oproj_kernel(oa_ref, ob_ref, ga_ref, gb_ref, w_ref, x_ref, gp_ref, h_ref, mix_ref, y_ref, *, nj, tn):
    j = pl.program_id(1)

    @pl.when(j == 0)
    def _():
        def norm_rows(rows):
            mix_ref[rows, :MIX_A] = _rms(oa_ref[rows, :].astype(F32), ga_ref[...]).astype(BF16)
            ob = jnp.concatenate([ob_ref[g, rows, :] for g in range(SWA_GROUP)], axis=1).astype(F32)
            mix_ref[rows, MIX_A:] = _rms(ob, gb_ref[...]).astype(BF16)
        _for_row_chunks(mix_ref.shape[0], norm_rows)

    y_ref[j] = jnp.dot(mix_ref[...], w_ref[...], preferred_element_type=F32)

    @pl.when(j == nj - 1)
    def _():
        def finish_rows(rows):
            ssq = jnp.zeros((ROW_CHUNK, 1), F32)
            for jj in range(nj):
                y = y_ref[jj, rows, :]
                ssq = ssq + jnp.sum(y * y, axis=-1, keepdims=True)
            r = lax.rsqrt(ssq * (1.0 / D_MODEL) + EPS)
            for jj in range(nj):
                cs = slice(jj * tn, (jj + 1) * tn)
                h_ref[rows, cs] = x_ref[rows, cs] + y_ref[jj, rows, :] * r * gp_ref[:, cs]
        _for_row_chunks(h_ref.shape[0], finish_rows)


def _out_proj(o_a, o_b, ga, gb, w_o_r, x2, gpost, tm=256, tn=512):
    T = x2.shape[0]
    nj = D_MODEL // tn
    return pl.pallas_call(
        functools.partial(_oproj_kernel, nj=nj, tn=tn),
        grid=(T // tm, nj),
        in_specs=[
            pl.BlockSpec((tm, MIX_A), lambda i, j: (i, 0)),
            pl.BlockSpec((SWA_GROUP, tm, KV_W), lambda i, j: (0, i, 0)),
            pl.BlockSpec((1, MIX_A), lambda i, j: (0, 0)),
            pl.BlockSpec((1, MIX_B), lambda i, j: (0, 0)),
            pl.BlockSpec((MIX_A + MIX_B, tn), lambda i, j: (0, j)),
            pl.BlockSpec((tm, D_MODEL), lambda i, j: (i, 0)),
            pl.BlockSpec((1, D_MODEL), lambda i, j: (0, 0)),
        ],
        out_specs=pl.BlockSpec((tm, D_MODEL), lambda i, j: (i, 0)),
        out_shape=jax.ShapeDtypeStruct((T, D_MODEL), F32),
        scratch_shapes=[
            pltpu.VMEM((tm, MIX_A + MIX_B), BF16),
            pltpu.VMEM((nj, tm, tn), F32),
        ],
        compiler_params=_params(("parallel", "arbitrary")),
        name="out_proj",
    )(o_a, o_b, ga, gb, w_o_r, x2, gpost)


def _ffn_kernel(h_ref, gpre_ref, wg_ref, wu_ref, wd_ref, gpost_ref, o_ref, f_ref, *, nj):
    j = pl.program_id(1)

    @pl.when(j == 0)
    def _():
        def norm_rows(rows):
            f_ref[rows, :] = _rms(h_ref[rows, :], gpre_ref[...]).astype(BF16)
            o_ref[rows, :] = jnp.zeros((ROW_CHUNK, D_MODEL), F32)
        _for_row_chunks(f_ref.shape[0], norm_rows)

    f = f_ref[...]
    gate = jnp.dot(f, wg_ref[...], preferred_element_type=F32)
    up = jnp.dot(f, wu_ref[...], preferred_element_type=F32)
    act = (gate * jax.nn.sigmoid(gate) * up).astype(BF16)
    for c in range(D_MODEL // DOWN_CHUNK):
        cs = slice(c * DOWN_CHUNK, (c + 1) * DOWN_CHUNK)
        o_ref[:, cs] += jnp.dot(act, wd_ref[:, cs], preferred_element_type=F32)

    @pl.when(j == nj - 1)
    def _():
        def finish_rows(rows):
            o_ref[rows, :] = h_ref[rows, :] + _rms(o_ref[rows, :], gpost_ref[...])
        _for_row_chunks(o_ref.shape[0], finish_rows)


def _ffn(h, gpre, wg, wu, wd, gpost, tm=512, tf=256):
    T = h.shape[0]
    nj = D_FF // tf
    return pl.pallas_call(
        functools.partial(_ffn_kernel, nj=nj),
        grid=(T // tm, nj),
        in_specs=[
            pl.BlockSpec((tm, D_MODEL), lambda i, j: (i, 0)),
            pl.BlockSpec((1, D_MODEL), lambda i, j: (0, 0)),
            pl.BlockSpec((D_MODEL, tf), lambda i, j: (0, j)),
            pl.BlockSpec((D_MODEL, tf), lambda i, j: (0, j)),
            pl.BlockSpec((tf, D_MODEL), lambda i, j: (j, 0)),
            pl.BlockSpec((1, D_MODEL), lambda i, j: (0, 0)),
        ],
        out_specs=pl.BlockSpec((tm, D_MODEL), lambda i, j: (i, 0)),
        out_shape=jax.ShapeDtypeStruct((T, D_MODEL), F32),
        scratch_shapes=[pltpu.VMEM((tm, D_MODEL), BF16)],
        compiler_params=_params(("parallel", "arbitrary")),
        name="ffn",
    )(h, gpre, wg, wu, wd, gpost)


def _swap_halves(w):
    half = w.shape[-1] // 2
    return jnp.concatenate([w[..., half:], w[..., :half]], axis=-1)


def _prep_w_in(w_in):
    c0 = MLA_Q_RANK + MLA_KV_RANK
    c1 = c0 + MLA_ROPE
    c2 = c1 + MIX_B
    kr = w_in[:, c0:c1]
    qs = w_in[:, c1:c2].reshape(D_MODEL, SWA_KV_HEADS, SWA_GROUP, SWA_HEAD_DIM)
    qs = qs.transpose(0, 2, 1, 3).reshape(D_MODEL, MIX_B)
    w_main = jnp.concatenate([w_in[:, :c0], qs, w_in[:, c2:]], axis=1).astype(BF16)
    w_rope = jnp.concatenate([kr, _swap_halves(kr)], axis=1).astype(BF16)
    return w_main, w_rope


def _prep_w_uq(w_uq):
    w = w_uq.reshape(MLA_Q_RANK, MLA_HEADS, MLA_QK)
    pe = w[..., MLA_NOPE:]
    w = jnp.concatenate([w[..., :MLA_NOPE], pe, _swap_halves(pe)], axis=-1)
    return w.reshape(MLA_Q_RANK, MLA_HEADS * QK_PAD).astype(BF16)


def _prep_w_ukv(w_ukv):
    w = w_ukv.reshape(MLA_KV_RANK, MLA_HEADS, MLA_NOPE + MLA_V)
    kn = w[..., :MLA_NOPE].reshape(MLA_KV_RANK, MLA_HEADS * MLA_NOPE)
    v = w[..., MLA_NOPE:].reshape(MLA_KV_RANK, MLA_HEADS * MLA_V)
    return jnp.concatenate([kn, v], axis=1).astype(BF16)


def _swa_perm_rows(a):
    rest = a.shape[1:]
    a = a.reshape((SWA_KV_HEADS, SWA_GROUP, SWA_HEAD_DIM) + rest)
    return jnp.swapaxes(a, 0, 1).reshape((MIX_B,) + rest)


def kernel(x, positions, attn_pre_g, w_in, q_norm_g, w_uq, kv_norm_g, w_ukv, swa_sinks, grp_a_g, grp_b_g,
           w_o, attn_post_g, ffn_pre_g, w_gate, w_up, w_down, ffn_post_g):
    B, S, _ = x.shape
    T = B * S
    depth = w_in.shape[0]
    inv = 1.0 / (ROPE_THETA ** (jnp.arange(0, MLA_ROPE, 2, dtype=F32) / MLA_ROPE))
    zeros64 = jnp.zeros((MLA_ROPE,), F32)
    inv_l = jnp.concatenate([inv, inv, zeros64]).reshape(1, LANES)
    cmask = jnp.concatenate([jnp.ones((MLA_ROPE,), F32), zeros64]).reshape(1, LANES)
    half = MLA_ROPE // 2
    sgn = jnp.concatenate([-jnp.ones((half,), F32), jnp.ones((half,), F32), zeros64]).reshape(1, LANES)
    slopes = jnp.exp2(-8.0 * jnp.arange(1, SWA_Q_HEADS + 1, dtype=F32) / SWA_Q_HEADS)
    pos_col = positions.reshape(T, 1)
    pos_row = positions.reshape(B, 1, S)

    h = x.reshape(T, D_MODEL)
    for l in range(depth):
        w_main, w_rope = _prep_w_in(w_in[l])
        w_o_r = jnp.concatenate([w_o[l][:MIX_A], _swa_perm_rows(w_o[l][MIX_A:])], axis=0).astype(BF16)
        gb = _swa_perm_rows(grp_b_g[l]).reshape(1, MIX_B)

        proj, kpe, cos_t, sin_t = _in_proj(h, attn_pre_g[l].reshape(1, D_MODEL), w_main, w_rope,
                                           pos_col, inv_l, cmask, sgn)
        q = _q_up(proj, q_norm_g[l].reshape(1, MLA_Q_RANK), _prep_w_uq(w_uq[l]), cos_t, sin_t)
        kv = _kv_up(proj, kv_norm_g[l].reshape(1, MLA_KV_RANK), _prep_w_ukv(w_ukv[l]))
        o_a = _mla_attention(q, kv, kpe, B, S)
        o_b = _swa_attention(proj, pos_col, pos_row, slopes, swa_sinks[l].astype(F32), B, S)
        h = _out_proj(o_a, o_b, grp_a_g[l].reshape(1, MIX_A), gb, w_o_r, h,
                      attn_post_g[l].reshape(1, D_MODEL))
        h = _ffn(h, ffn_pre_g[l].reshape(1, D_MODEL), w_gate[l].astype(BF16), w_up[l].astype(BF16),
                 w_down[l].astype(BF16), ffn_post_g[l].reshape(1, D_MODEL))
    return h.reshape(B, S, D_MODEL)
```

```python
import functools

import jax
import jax.numpy as jnp
from jax import lax
from jax.experimental import pallas as pl
from jax.experimental.pallas import tpu as pltpu

D_MODEL = 4096
MLA_HEADS = 16
MLA_Q_RANK = 1024
MLA_KV_RANK = 512
MLA_NOPE = 128
MLA_ROPE = 64
MLA_V = 128
MLA_QK = MLA_NOPE + MLA_ROPE
ROPE_THETA = 10000.0
SWA_Q_HEADS = 32
SWA_KV_HEADS = 8
SWA_HEAD_DIM = 64
SWA_GROUP = SWA_Q_HEADS // SWA_KV_HEADS
SWA_WINDOW = 128
BLOCK = 128
MIX_A = MLA_HEADS * MLA_V
MIX_B = SWA_Q_HEADS * SWA_HEAD_DIM
D_FF = 11008
EPS = 1e-6

LANES = 128
QK_PAD = 256
KV_W = SWA_KV_HEADS * SWA_HEAD_DIM
LATENT_W = MLA_Q_RANK + MLA_KV_RANK
MLA_IN_W = LATENT_W + LANES
COL_QS = 0
COL_KS = COL_QS + MIX_B
COL_VS = COL_KS + KV_W
SWA_IN_W = COL_VS + KV_W
NEG = float(jnp.finfo(jnp.float32).min)
VMEM_LIMIT = 56 * 1024 * 1024
DOWN_CHUNK = 1024
N_CHUNK = 1024
ROW_CHUNK = 64
LOG2_E = 1.4426950408889634
MASKED_DIST = 1e30
assert COL_KS % N_CHUNK == 0

F32 = jnp.float32
BF16 = jnp.bfloat16


def _rms(xf, g):
    return xf * lax.rsqrt(jnp.mean(xf * xf, axis=-1, keepdims=True) + EPS) * g


def _row_rsqrt(xf):
    return lax.rsqrt(jnp.mean(xf * xf, axis=-1, keepdims=True) + EPS)


def _for_row_chunks(n_rows, fn, chunk=ROW_CHUNK):
    def body(r, carry):
        fn(pl.ds(pl.multiple_of(r * chunk, chunk), chunk))
        return carry
    lax.fori_loop(0, n_rows // chunk, body, 0)


def _params(sem):
    return pltpu.CompilerParams(dimension_semantics=sem, vmem_limit_bytes=VMEM_LIMIT)


def _resident(shape):
    return pl.BlockSpec(shape, lambda *_: (0,) * len(shape), pipeline_mode=pl.Buffered(1))


def _col_chunks(width, chunk=N_CHUNK):
    return [slice(lo, min(lo + chunk, width)) for lo in range(0, width, chunk)]


def _inproj_mla_kernel(x_ref, g_ref, w_ref, pos_ref, inv_ref, cmask_ref, sgn_ref,
                       lat_ref, kpe_ref, cos_ref, sin_ref):
    x = x_ref[...]
    r = _row_rsqrt(x)
    xg = (x * g_ref[...]).astype(BF16)
    lat_ref[...] = (jnp.dot(xg, w_ref[:, :LATENT_W], preferred_element_type=F32) * r).astype(BF16)
    ang = pos_ref[...].astype(F32) * inv_ref[...]
    c = jnp.cos(ang) * cmask_ref[...]
    s = jnp.sin(ang) * sgn_ref[...]
    cos_ref[...] = c
    sin_ref[...] = s
    kr = jnp.dot(xg, w_ref[:, LATENT_W:], preferred_element_type=F32) * r
    kpe_ref[...] = (kr * c + pltpu.roll(kr, 64, 1) * s).astype(BF16)


def _in_proj_mla(x2, g, w_mla, pos_col, inv, cmask, sgn, tm=256):
    T = x2.shape[0]
    row = lambda i: (i, 0)
    return pl.pallas_call(
        _inproj_mla_kernel,
        grid=(T // tm,),
        in_specs=[
            pl.BlockSpec((tm, D_MODEL), row),
            _resident((1, D_MODEL)),
            _resident((D_MODEL, MLA_IN_W)),
            pl.BlockSpec((tm, 1), row),
            _resident((1, LANES)),
            _resident((1, LANES)),
            _resident((1, LANES)),
        ],
        out_specs=[
            pl.BlockSpec((tm, LATENT_W), row),
            pl.BlockSpec((tm, LANES), row),
            pl.BlockSpec((tm, LANES), row),
            pl.BlockSpec((tm, LANES), row),
        ],
        out_shape=[
            jax.ShapeDtypeStruct((T, LATENT_W), BF16),
            jax.ShapeDtypeStruct((T, LANES), BF16),
            jax.ShapeDtypeStruct((T, LANES), F32),
            jax.ShapeDtypeStruct((T, LANES), F32),
        ],
        compiler_params=_params(("parallel",)),
        name="in_proj_mla",
    )(x2, g, w_mla, pos_col, inv, cmask, sgn)


def _inproj_swa_kernel(x_ref, g_ref, w_ref, o_ref):
    x = x_ref[...]
    r = _row_rsqrt(x)
    xg = (x * g_ref[...]).astype(BF16)
    rq = r * (SWA_HEAD_DIM ** -0.5 * LOG2_E)
    for cs in _col_chunks(SWA_IN_W):
        rc = rq if cs.stop <= COL_KS else r
        o_ref[:, cs] = (jnp.dot(xg, w_ref[:, cs], preferred_element_type=F32) * rc).astype(BF16)


def _in_proj_swa(x2, g, w_swa, tm=256):
    T = x2.shape[0]
    row = lambda i: (i, 0)
    return pl.pallas_call(
        _inproj_swa_kernel,
        grid=(T // tm,),
        in_specs=[
            pl.BlockSpec((tm, D_MODEL), row),
            _resident((1, D_MODEL)),
            _resident((D_MODEL, SWA_IN_W)),
        ],
        out_specs=pl.BlockSpec((tm, SWA_IN_W), row),
        out_shape=jax.ShapeDtypeStruct((T, SWA_IN_W), BF16),
        compiler_params=_params(("parallel",)),
        name="in_proj_swa",
    )(x2, g, w_swa)


def _qkvup_kernel(cq_ref, ckv_ref, gq_ref, gkv_ref, wq_ref, wkv_ref, cos_ref, sin_ref, q_ref, kv_ref):
    scale = MLA_QK ** -0.5 * LOG2_E
    cq = cq_ref[...].astype(F32)
    rq = _row_rsqrt(cq) * scale
    cqg = (cq * gq_ref[...]).astype(BF16)
    c = cos_ref[...]
    s = sin_ref[...]
    for cs in _col_chunks(MLA_HEADS * QK_PAD):
        y = jnp.dot(cqg, wq_ref[:, cs], preferred_element_type=F32) * rq
        for h in range((cs.stop - cs.start) // QK_PAD):
            lo = h * QK_PAD
            q_ref[:, cs.start + lo:cs.start + lo + MLA_NOPE] = y[:, lo:lo + MLA_NOPE].astype(BF16)
            pe = y[:, lo + MLA_NOPE:lo + QK_PAD]
            q_ref[:, cs.start + lo + MLA_NOPE:cs.start + lo + QK_PAD] = (
                pe * c + pltpu.roll(pe, 64, 1) * s).astype(BF16)
    ckv = ckv_ref[...].astype(F32)
    rkv = _row_rsqrt(ckv)
    ckvg = (ckv * gkv_ref[...]).astype(BF16)
    for cs in _col_chunks(MLA_HEADS * (MLA_NOPE + MLA_V)):
        kv_ref[:, cs] = (jnp.dot(ckvg, wkv_ref[:, cs], preferred_element_type=F32) * rkv).astype(BF16)


def _qkv_up(lat, gq, gkv, w_uq_r, w_ukv_r, cos_t, sin_t, tm=256):
    T = lat.shape[0]
    nq = MLA_HEADS * QK_PAD
    nkv = MLA_HEADS * (MLA_NOPE + MLA_V)
    row = lambda i: (i, 0)
    return pl.pallas_call(
        _qkvup_kernel,
        grid=(T // tm,),
        in_specs=[
            pl.BlockSpec((tm, MLA_Q_RANK), row),
            pl.BlockSpec((tm, MLA_KV_RANK), lambda i: (i, MLA_Q_RANK // MLA_KV_RANK)),
            _resident((1, MLA_Q_RANK)),
            _resident((1, MLA_KV_RANK)),
            _resident((MLA_Q_RANK, nq)),
            _resident((MLA_KV_RANK, nkv)),
            pl.BlockSpec((tm, LANES), row),
            pl.BlockSpec((tm, LANES), row),
        ],
        out_specs=[pl.BlockSpec((tm, nq), row), pl.BlockSpec((tm, nkv), row)],
        out_shape=[jax.ShapeDtypeStruct((T, nq), BF16), jax.ShapeDtypeStruct((T, nkv), BF16)],
        compiler_params=_params(("parallel",)),
        name="qkv_up",
    )(lat, lat, gq, gkv, w_uq_r, w_ukv_r, cos_t, sin_t)


def _mla_kernel(q_ref, kn_ref, kpe_ref, v_ref, o_ref, m_ref, l_ref, acc_ref, *, th, heads):
    qi = pl.program_id(2)
    m_ref[...] = jnp.full(m_ref.shape, NEG, F32)
    l_ref[...] = jnp.zeros(l_ref.shape, F32)
    acc_ref[...] = jnp.zeros(acc_ref.shape, F32)

    def load_kv(head, start, size):
        rows = pl.ds(pl.multiple_of(start, size), size)
        k = jnp.concatenate([kn_ref[rows, head * MLA_NOPE:(head + 1) * MLA_NOPE], kpe_ref[rows, :]], axis=1)
        return k, v_ref[rows, head * MLA_V:(head + 1) * MLA_V]

    def update(head, half, k, v, diag_col=None):
        chain = 2 * head + half
        q = q_ref[half * th:(half + 1) * th, head * QK_PAD:(head + 1) * QK_PAD]
        s = lax.dot_general(q, k, (((1,), (1,)), ((), ())), preferred_element_type=F32)
        chunks = [s[:, c * LANES:(c + 1) * LANES] for c in range(s.shape[1] // LANES)]
        if diag_col is not None:
            row = lax.broadcasted_iota(jnp.int32, (th, LANES), 0)
            col = lax.broadcasted_iota(jnp.int32, (th, LANES), 1)
            for c in range(diag_col // LANES, len(chunks)):
                chunks[c] = jnp.where(col + (c * LANES - diag_col) <= row, chunks[c], NEG)
        mx = chunks[0]
        for c in chunks[1:]:
            mx = jnp.maximum(mx, c)
        m_prev = m_ref[chain]
        m_new = jnp.maximum(m_prev, jnp.max(mx, axis=-1, keepdims=True))
        alpha = jnp.exp2(m_prev - m_new)
        ps = [jnp.exp2(c - m_new) for c in chunks]
        psum = ps[0]
        for p in ps[1:]:
            psum = psum + p
        l_ref[chain] = alpha * l_ref[chain] + psum
        p = jnp.concatenate(ps, axis=1).astype(BF16)
        acc_ref[chain] = alpha * acc_ref[chain] + jnp.dot(p, v, preferred_element_type=F32)
        m_ref[chain] = m_new

    def body(j, carry):
        for head in range(heads):
            k, v = load_kv(head, j * tq, tq)
            update(head, 0, k, v)
            update(head, 1, k, v)
        return carry

    tq = 2 * th
    lax.fori_loop(0, qi, body, 0)
    for head in range(heads):
        k, v = load_kv(head, qi * tq, tq)
        update(head, 0, k[:th], v[:th], diag_col=0)
        update(head, 1, k, v, diag_col=th)
    for head in range(heads):
        for half in range(2):
            l = jnp.sum(l_ref[2 * head + half], axis=-1, keepdims=True)
            o_ref[half * th:(half + 1) * th, head * MLA_V:(head + 1) * MLA_V] = (
                acc_ref[2 * head + half] / l).astype(BF16)


def _mla_attention(q, kv, kpe, B, S, th=512, heads=2):
    T = B * S
    tq = 2 * th
    nq = S // tq
    n_groups = MLA_HEADS // heads
    return pl.pallas_call(
        functools.partial(_mla_kernel, th=th, heads=heads),
        grid=(B, n_groups, nq),
        in_specs=[
            pl.BlockSpec((tq, heads * QK_PAD), lambda b, h, i: (b * nq + i, h)),
            pl.BlockSpec((S, heads * MLA_NOPE), lambda b, h, i: (b, h)),
            pl.BlockSpec((S, LANES), lambda b, h, i: (b, 0)),
            pl.BlockSpec((S, heads * MLA_V), lambda b, h, i: (b, n_groups + h)),
        ],
        out_specs=pl.BlockSpec((tq, heads * MLA_V), lambda b, h, i: (b * nq + i, h)),
        out_shape=jax.ShapeDtypeStruct((T, MIX_A), BF16),
        scratch_shapes=[
            pltpu.VMEM((2 * heads, th, LANES), F32),
            pltpu.VMEM((2 * heads, th, LANES), F32),
            pltpu.VMEM((2 * heads, th, MLA_V), F32),
        ],
        compiler_params=_params(("parallel", "parallel", "arbitrary")),
        name="mla_attn",
    )(q, kv, kpe, kv)


def _swa_kernel(slope_ref, sink_ref, q0_ref, q1_ref, q2_ref, q3_ref, k_ref, kp_ref, v_ref, vp_ref,
                pc_ref, pr_ref, prp_ref, o_ref, *, tq):
    n = pl.program_id(1)
    pair = pl.program_id(2)
    q_refs = (q0_ref, q1_ref, q2_ref, q3_ref)
    lane = lax.broadcasted_iota(jnp.int32, (2 * BLOCK, LANES), 1)
    lo = lane < SWA_HEAD_DIM
    k_all = jnp.concatenate([kp_ref[...], k_ref[...]], axis=0)
    v_all = jnp.concatenate([vp_ref[...], v_ref[...]], axis=0)
    pos_k = jnp.concatenate([prp_ref[0], pr_ref[0]], axis=1)
    i_idx = lax.broadcasted_iota(jnp.int32, (BLOCK, 2 * BLOCK), 0)
    c_idx = lax.broadcasted_iota(jnp.int32, (BLOCK, 2 * BLOCK), 1)
    in_window = (c_idx > i_idx) & (c_idx <= i_idx + SWA_WINDOW)
    zero = jnp.zeros((), BF16)
    lo_out = lax.broadcasted_iota(jnp.int32, (BLOCK, LANES), 1) < SWA_HEAD_DIM

    for t in range(tq // BLOCK):
        r0 = t * BLOCK
        band_k = k_all[r0:r0 + 2 * BLOCK]
        band_v = v_all[r0:r0 + 2 * BLOCK]
        k_bd = jnp.concatenate([jnp.where(lo, band_k, zero), jnp.where(lo, zero, band_k)], axis=0)
        v_bd = jnp.concatenate([jnp.where(lo, band_v, zero), jnp.where(lo, zero, band_v)], axis=0)
        q_st = jnp.concatenate([qr[r0:r0 + BLOCK, :] for qr in q_refs], axis=0)
        s = lax.dot_general(q_st, k_bd, (((1,), (1,)), ((), ())), preferred_element_type=F32)
        dist = jnp.abs(pc_ref[r0:r0 + BLOCK, :] - pos_k[:, r0:r0 + 2 * BLOCK]).astype(F32)
        valid = in_window
        if t == 0:
            valid = valid & ((c_idx >= BLOCK) | (n > 0))
        dist = jnp.where(valid, dist, MASKED_DIST)
        rows = []
        inv_den = []
        for g in range(SWA_GROUP):
            halves = []
            for c in range(2):
                head = (2 * pair + c) * SWA_GROUP + g
                slope = slope_ref[head] * LOG2_E
                sink = sink_ref[head] * LOG2_E
                sg = s[g * BLOCK:(g + 1) * BLOCK, c * 2 * BLOCK:(c + 1) * 2 * BLOCK] - slope * dist
                m = jnp.maximum(jnp.max(sg, axis=-1, keepdims=True), sink)
                e = jnp.exp2(sg - m)
                den = jnp.sum(e, axis=-1, keepdims=True) + jnp.exp2(sink - m)
                halves.append(e.astype(BF16))
                inv_den.append(1.0 / den)
            rows.append(jnp.concatenate(halves, axis=1))
        p = jnp.concatenate(rows, axis=0)
        out = jnp.dot(p, v_bd, preferred_element_type=F32)
        for g in range(SWA_GROUP):
            norm = jnp.where(lo_out, inv_den[2 * g], inv_den[2 * g + 1])
            o_ref[g, r0:r0 + BLOCK, :] = (out[g * BLOCK:(g + 1) * BLOCK] * norm).astype(BF16)


def _swa_attention(proj, pos_col, pos_row, slopes, sinks, B, S, tq=512):
    T = B * S
    nt = S // tq
    sub = tq // BLOCK
    npairs = SWA_KV_HEADS // 2
    qs0 = COL_QS // LANES
    ks0 = COL_KS // LANES
    vs0 = COL_VS // LANES
    pairs_per_group = KV_W // LANES

    def q_spec(g):
        return pl.BlockSpec((tq, LANES), lambda b, n, j, g=g: (b * nt + n, qs0 + pairs_per_group * g + j))

    def prev_row(b, n):
        return jnp.maximum(b * (S // BLOCK) + sub * n - 1, 0)

    smem = pl.BlockSpec(memory_space=pltpu.SMEM)
    return pl.pallas_call(
        functools.partial(_swa_kernel, tq=tq),
        grid=(B, nt, npairs),
        in_specs=[
            smem, smem,
            q_spec(0), q_spec(1), q_spec(2), q_spec(3),
            pl.BlockSpec((tq, LANES), lambda b, n, j: (b * nt + n, ks0 + j)),
            pl.BlockSpec((BLOCK, LANES), lambda b, n, j: (prev_row(b, n), ks0 + j)),
            pl.BlockSpec((tq, LANES), lambda b, n, j: (b * nt + n, vs0 + j)),
            pl.BlockSpec((BLOCK, LANES), lambda b, n, j: (prev_row(b, n), vs0 + j)),
            pl.BlockSpec((tq, 1), lambda b, n, j: (b * nt + n, 0)),
            pl.BlockSpec((1, 1, tq), lambda b, n, j: (b, 0, n)),
            pl.BlockSpec((1, 1, BLOCK), lambda b, n, j: (b, 0, jnp.maximum(sub * n - 1, 0))),
        ],
        out_specs=pl.BlockSpec((SWA_GROUP, tq, LANES), lambda b, n, j: (0, b * nt + n, j)),
        out_shape=jax.ShapeDtypeStruct((SWA_GROUP, T, KV_W), BF16),
        compiler_params=_params(("parallel", "parallel", "arbitrary")),
        name="swa_attn",
    )(slopes, sinks, proj, proj, proj, proj, proj, proj, proj, proj, pos_col, pos_row, pos_row)


def _oproj_a_kernel(oa_ref, ga_ref, w_ref, y_ref):
    oa = oa_ref[...].astype(F32)
    r = _row_rsqrt(oa)
    og = (oa * ga_ref[...]).astype(BF16)
    for cs in _col_chunks(D_MODEL):
        y_ref[:, cs] = jnp.dot(og, w_ref[:, cs], preferred_element_type=F32) * r


def _out_proj_a(o_a, ga, w_oa, tm=256):
    T = o_a.shape[0]
    row = lambda i: (i, 0)
    return pl.pallas_call(
        _oproj_a_kernel,
        grid=(T // tm,),
        in_specs=[pl.BlockSpec((tm, MIX_A), row), _resident((1, MIX_A)), _resident((MIX_A, D_MODEL))],
        out_specs=pl.BlockSpec((tm, D_MODEL), row),
        out_shape=jax.ShapeDtypeStruct((T, D_MODEL), F32),
        compiler_params=_params(("parallel",)),
        name="out_proj_a",
    )(o_a, ga, w_oa)


def _oproj_b_kernel(ob_ref, gb_ref, w_ref, ya_ref, x_ref, gp_ref, h_ref):
    ob = jnp.concatenate([ob_ref[g] for g in range(SWA_GROUP)], axis=1).astype(F32)
    r = _row_rsqrt(ob)
    og = (ob * gb_ref[...]).astype(BF16)
    ssq = jnp.zeros((h_ref.shape[0], 1), F32)
    for cs in _col_chunks(D_MODEL):
        y = ya_ref[:, cs] + jnp.dot(og, w_ref[:, cs], preferred_element_type=F32) * r
        ssq = ssq + jnp.sum(y * y, axis=-1, keepdims=True)
        h_ref[:, cs] = y
    rn = lax.rsqrt(ssq * (1.0 / D_MODEL) + EPS)
    for cs in _col_chunks(D_MODEL):
        h_ref[:, cs] = x_ref[:, cs] + h_ref[:, cs] * rn * gp_ref[:, cs]


def _out_proj_b(o_b, gb, w_ob, y_a, x2, gpost, tm=256):
    T = x2.shape[0]
    row = lambda i: (i, 0)
    return pl.pallas_call(
        _oproj_b_kernel,
        grid=(T // tm,),
        in_specs=[
            pl.BlockSpec((SWA_GROUP, tm, KV_W), lambda i: (0, i, 0)),
            _resident((1, MIX_B)),
            _resident((MIX_B, D_MODEL)),
            pl.BlockSpec((tm, D_MODEL), row),
            pl.BlockSpec((tm, D_MODEL), row),
            _resident((1, D_MODEL)),
        ],
        out_specs=pl.BlockSpec((tm, D_MODEL), row),
        out_shape=jax.ShapeDtypeStruct((T, D_MODEL), F32),
        compiler_params=_params(("parallel",)),
        name="out_proj_b",
    )(o_b, gb, w_ob, y_a, x2, gpost)


def _ffn_kernel(h_ref, gpre_ref, wg_ref, wu_ref, wd_ref, gpost_ref, o_ref, f_ref, *, nj):
    j = pl.program_id(1)

    @pl.when(j == 0)
    def _():
        def norm_rows(rows):
            f_ref[rows, :] = _rms(h_ref[rows, :], gpre_ref[...]).astype(BF16)
            o_ref[rows, :] = jnp.zeros((ROW_CHUNK, D_MODEL), F32)
        _for_row_chunks(f_ref.shape[0], norm_rows)

    f = f_ref[...]
    gate = jnp.dot(f, wg_ref[...], preferred_element_type=F32)
    up = jnp.dot(f, wu_ref[...], preferred_element_type=F32)
    act = (gate * jax.nn.sigmoid(gate) * up).astype(BF16)
    for c in range(D_MODEL // DOWN_CHUNK):
        cs = slice(c * DOWN_CHUNK, (c + 1) * DOWN_CHUNK)
        o_ref[:, cs] += jnp.dot(act, wd_ref[:, cs], preferred_element_type=F32)

    @pl.when(j == nj - 1)
    def _():
        def finish_rows(rows):
            o_ref[rows, :] = h_ref[rows, :] + _rms(o_ref[rows, :], gpost_ref[...])
        _for_row_chunks(o_ref.shape[0], finish_rows)


def _ffn(h, gpre, wg, wu, wd, gpost, tm=512, tf=256):
    T = h.shape[0]
    nj = D_FF // tf
    return pl.pallas_call(
        functools.partial(_ffn_kernel, nj=nj),
        grid=(T // tm, nj),
        in_specs=[
            pl.BlockSpec((tm, D_MODEL), lambda i, j: (i, 0)),
            pl.BlockSpec((1, D_MODEL), lambda i, j: (0, 0)),
            pl.BlockSpec((D_MODEL, tf), lambda i, j: (0, j)),
            pl.BlockSpec((D_MODEL, tf), lambda i, j: (0, j)),
            pl.BlockSpec((tf, D_MODEL), lambda i, j: (j, 0)),
            pl.BlockSpec((1, D_MODEL), lambda i, j: (0, 0)),
        ],
        out_specs=pl.BlockSpec((tm, D_MODEL), lambda i, j: (i, 0)),
        out_shape=jax.ShapeDtypeStruct((T, D_MODEL), F32),
        scratch_shapes=[pltpu.VMEM((tm, D_MODEL), BF16)],
        compiler_params=_params(("parallel", "arbitrary")),
        name="ffn",
    )(h, gpre, wg, wu, wd, gpost)


def _swap_halves(w):
    half = w.shape[-1] // 2
    return jnp.concatenate([w[..., half:], w[..., :half]], axis=-1)


def _prep_w_in(w_in):
    c0 = LATENT_W
    c1 = c0 + MLA_ROPE
    c2 = c1 + MIX_B
    kr = w_in[:, c0:c1]
    w_mla = jnp.concatenate([w_in[:, :c0], kr, _swap_halves(kr)], axis=1).astype(BF16)
    qs = w_in[:, c1:c2].reshape(D_MODEL, SWA_KV_HEADS, SWA_GROUP, SWA_HEAD_DIM)
    qs = qs.transpose(0, 2, 1, 3).reshape(D_MODEL, MIX_B)
    w_swa = jnp.concatenate([qs, w_in[:, c2:]], axis=1).astype(BF16)
    return w_mla, w_swa


def _prep_w_uq(w_uq):
    w = w_uq.reshape(MLA_Q_RANK, MLA_HEADS, MLA_QK)
    pe = w[..., MLA_NOPE:]
    w = jnp.concatenate([w[..., :MLA_NOPE], pe, _swap_halves(pe)], axis=-1)
    return w.reshape(MLA_Q_RANK, MLA_HEADS * QK_PAD).astype(BF16)


def _prep_w_ukv(w_ukv):
    w = w_ukv.reshape(MLA_KV_RANK, MLA_HEADS, MLA_NOPE + MLA_V)
    kn = w[..., :MLA_NOPE].reshape(MLA_KV_RANK, MLA_HEADS * MLA_NOPE)
    v = w[..., MLA_NOPE:].reshape(MLA_KV_RANK, MLA_HEADS * MLA_V)
    return jnp.concatenate([kn, v], axis=1).astype(BF16)


def _swa_perm_rows(a):
    rest = a.shape[1:]
    a = a.reshape((SWA_KV_HEADS, SWA_GROUP, SWA_HEAD_DIM) + rest)
    return jnp.swapaxes(a, 0, 1).reshape((MIX_B,) + rest)


def kernel(x, positions, attn_pre_g, w_in, q_norm_g, w_uq, kv_norm_g, w_ukv, swa_sinks, grp_a_g, grp_b_g,
           w_o, attn_post_g, ffn_pre_g, w_gate, w_up, w_down, ffn_post_g):
    B, S, _ = x.shape
    T = B * S
    depth = w_in.shape[0]
    inv = 1.0 / (ROPE_THETA ** (jnp.arange(0, MLA_ROPE, 2, dtype=F32) / MLA_ROPE))
    zeros64 = jnp.zeros((MLA_ROPE,), F32)
    inv_l = jnp.concatenate([inv, inv, zeros64]).reshape(1, LANES)
    cmask = jnp.concatenate([jnp.ones((MLA_ROPE,), F32), zeros64]).reshape(1, LANES)
    half = MLA_ROPE // 2
    sgn = jnp.concatenate([-jnp.ones((half,), F32), jnp.ones((half,), F32), zeros64]).reshape(1, LANES)
    slopes = jnp.exp2(-8.0 * jnp.arange(1, SWA_Q_HEADS + 1, dtype=F32) / SWA_Q_HEADS)
    pos_col = positions.reshape(T, 1)
    pos_row = positions.reshape(B, 1, S)

    h = x.reshape(T, D_MODEL)
    for l in range(depth):
        w_mla, w_swa = _prep_w_in(w_in[l])
        w_oa = w_o[l][:MIX_A].astype(BF16)
        w_ob = _swa_perm_rows(w_o[l][MIX_A:]).astype(BF16)
        gb = _swa_perm_rows(grp_b_g[l]).reshape(1, MIX_B)
        g_pre = attn_pre_g[l].reshape(1, D_MODEL)

        lat, kpe, cos_t, sin_t = _in_proj_mla(h, g_pre, w_mla, pos_col, inv_l, cmask, sgn)
        proj_swa = _in_proj_swa(h, g_pre, w_swa)
        q, kv = _qkv_up(lat, q_norm_g[l].reshape(1, MLA_Q_RANK), kv_norm_g[l].reshape(1, MLA_KV_RANK),
                        _prep_w_uq(w_uq[l]), _prep_w_ukv(w_ukv[l]), cos_t, sin_t)
        o_a = _mla_attention(q, kv, kpe, B, S)
        y_a = _out_proj_a(o_a, grp_a_g[l].reshape(1, MIX_A), w_oa)
        o_b = _swa_attention(proj_swa, pos_col, pos_row, slopes, swa_sinks[l].astype(F32), B, S)
        h = _out_proj_b(o_b, gb, w_ob, y_a, h, attn_post_g[l].reshape(1, D_MODEL))
        h = _ffn(h, ffn_pre_g[l].reshape(1, D_MODEL), w_gate[l].astype(BF16), w_up[l].astype(BF16),
                 w_down[l].astype(BF16), ffn_post_g[l].reshape(1, D_MODEL))
    return h.reshape(B, S, D_MODEL)
```

```python
import functools

import jax
import jax.numpy as jnp
from jax import lax
from jax.experimental import pallas as pl
from jax.experimental.pallas import tpu as pltpu

D_MODEL = 4096
MLA_HEADS = 16
MLA_Q_RANK = 1024
MLA_KV_RANK = 512
MLA_NOPE = 128
MLA_ROPE = 64
MLA_V = 128
MLA_QK = MLA_NOPE + MLA_ROPE
ROPE_THETA = 10000.0
SWA_Q_HEADS = 32
SWA_KV_HEADS = 8
SWA_HEAD_DIM = 64
SWA_GROUP = SWA_Q_HEADS // SWA_KV_HEADS
SWA_WINDOW = 128
BLOCK = 128
MIX_A = MLA_HEADS * MLA_V
MIX_B = SWA_Q_HEADS * SWA_HEAD_DIM
D_FF = 11008
EPS = 1e-6

LANES = 128
QK_PAD = 256
KV_W = SWA_KV_HEADS * SWA_HEAD_DIM
LATENT_W = MLA_Q_RANK + MLA_KV_RANK
COL_QS = 0
COL_KS = COL_QS + MIX_B
COL_VS = COL_KS + KV_W
SWA_IN_W = COL_VS + KV_W
NEG = float(jnp.finfo(jnp.float32).min)
VMEM_LIMIT = 56 * 1024 * 1024
DOWN_CHUNK = 1024
FF_TILE = 256
N_CHUNK = 1024
PREP_ROWS = 16
PREP_UNROLL = 4
FINISH_ROWS = 64
LOG2_E = 1.4426950408889634
MASKED_DIST = 1e30
F32 = jnp.float32
BF16 = jnp.bfloat16


def _rms(xf, g):
    return xf * lax.rsqrt(jnp.mean(xf * xf, axis=-1, keepdims=True) + EPS) * g


def _row_rsqrt(xf):
    return lax.rsqrt(jnp.mean(xf * xf, axis=-1, keepdims=True) + EPS)


def _for_row_chunks(n_rows, fn, chunk, unroll=1):
    def body(r, carry):
        fn(pl.ds(pl.multiple_of(r * chunk, chunk), chunk))
        return carry
    lax.fori_loop(0, n_rows // chunk, body, 0, unroll=unroll)


def _params(sem):
    return pltpu.CompilerParams(dimension_semantics=sem, vmem_limit_bytes=VMEM_LIMIT)


def _resident(shape):
    return pl.BlockSpec(shape, lambda *_: (0,) * len(shape), pipeline_mode=pl.Buffered(1))


def _col_chunks(width, chunk=N_CHUNK):
    return [slice(lo, min(lo + chunk, width)) for lo in range(0, width, chunk)]


def _inproj_mla_kernel(x_ref, g_ref, wl_ref, wr_ref, lat_ref, kr_ref):
    x = x_ref[...]
    r = _row_rsqrt(x)
    xg = (x * g_ref[...]).astype(BF16)
    lat_ref[...] = (jnp.dot(xg, wl_ref[...], preferred_element_type=F32) * r).astype(BF16)
    kr_ref[...] = jnp.dot(xg, wr_ref[...], preferred_element_type=F32) * r


def _in_proj_mla(x2, g, w_lat, w_rope, tm=512):
    T = x2.shape[0]
    row = lambda i: (i, 0)
    return pl.pallas_call(
        _inproj_mla_kernel,
        grid=(T // tm,),
        in_specs=[
            pl.BlockSpec((tm, D_MODEL), row),
            _resident((1, D_MODEL)),
            _resident((D_MODEL, LATENT_W)),
            _resident((D_MODEL, LANES)),
        ],
        out_specs=[pl.BlockSpec((tm, LATENT_W), row), pl.BlockSpec((tm, LANES), row)],
        out_shape=[jax.ShapeDtypeStruct((T, LATENT_W), BF16), jax.ShapeDtypeStruct((T, LANES), F32)],
        compiler_params=_params(("parallel",)),
        name="in_proj_mla",
    )(x2, g, w_lat, w_rope)


def _inproj_swa_kernel(x_ref, g_ref, wq_ref, wkv_ref, o_ref):
    x = x_ref[...]
    r = _row_rsqrt(x)
    xg = (x * g_ref[...]).astype(BF16)
    rq = r * (SWA_HEAD_DIM ** -0.5 * LOG2_E)
    for cs in _col_chunks(MIX_B):
        o_ref[:, cs] = (jnp.dot(xg, wq_ref[:, cs], preferred_element_type=F32) * rq).astype(BF16)
    o_ref[:, COL_KS:] = (jnp.dot(xg, wkv_ref[...], preferred_element_type=F32) * r).astype(BF16)


def _in_proj_swa(x2, g, w_q, w_kv, tm=256):
    T = x2.shape[0]
    row = lambda i: (i, 0)
    return pl.pallas_call(
        _inproj_swa_kernel,
        grid=(T // tm,),
        in_specs=[
            pl.BlockSpec((tm, D_MODEL), row),
            _resident((1, D_MODEL)),
            _resident((D_MODEL, MIX_B)),
            _resident((D_MODEL, 2 * KV_W)),
        ],
        out_specs=pl.BlockSpec((tm, SWA_IN_W), row),
        out_shape=jax.ShapeDtypeStruct((T, SWA_IN_W), BF16),
        compiler_params=_params(("parallel",)),
        name="in_proj_swa",
    )(x2, g, w_q, w_kv)


def _qkvup_kernel(cq_ref, ckv_ref, kr_ref, pos_ref, inv_ref, cmask_ref, sgn_ref, gq_ref, gkv_ref,
                  wq_ref, wkv_ref, q_ref, kv_ref, kpe_ref):
    ang = pos_ref[...].astype(F32) * inv_ref[...]
    c = jnp.cos(ang) * cmask_ref[...]
    s = jnp.sin(ang) * sgn_ref[...]
    kr = kr_ref[...]
    kpe_ref[...] = (kr * c + pltpu.roll(kr, 64, 1) * s).astype(BF16)
    scale = MLA_QK ** -0.5 * LOG2_E
    cq = cq_ref[...].astype(F32)
    rq = _row_rsqrt(cq) * scale
    cqg = (cq * gq_ref[...]).astype(BF16)
    for cs in _col_chunks(MLA_HEADS * QK_PAD):
        y = jnp.dot(cqg, wq_ref[:, cs], preferred_element_type=F32) * rq
        for h in range((cs.stop - cs.start) // QK_PAD):
            lo = h * QK_PAD
            q_ref[:, cs.start + lo:cs.start + lo + MLA_NOPE] = y[:, lo:lo + MLA_NOPE].astype(BF16)
            pe = y[:, lo + MLA_NOPE:lo + QK_PAD]
            q_ref[:, cs.start + lo + MLA_NOPE:cs.start + lo + QK_PAD] = (
                pe * c + pltpu.roll(pe, 64, 1) * s).astype(BF16)
    ckv = ckv_ref[...].astype(F32)
    rkv = _row_rsqrt(ckv)
    ckvg = (ckv * gkv_ref[...]).astype(BF16)
    for cs in _col_chunks(MLA_HEADS * (MLA_NOPE + MLA_V)):
        kv_ref[:, cs] = (jnp.dot(ckvg, wkv_ref[:, cs], preferred_element_type=F32) * rkv).astype(BF16)


def _qkv_up(lat, kr, pos_col, inv, cmask, sgn, gq, gkv, w_uq_r, w_ukv_r, tm=512):
    T = lat.shape[0]
    nq = MLA_HEADS * QK_PAD
    nkv = MLA_HEADS * (MLA_NOPE + MLA_V)
    row = lambda i: (i, 0)
    return pl.pallas_call(
        _qkvup_kernel,
        grid=(T // tm,),
        in_specs=[
            pl.BlockSpec((tm, MLA_Q_RANK), row),
            pl.BlockSpec((tm, MLA_KV_RANK), lambda i: (i, MLA_Q_RANK // MLA_KV_RANK)),
            pl.BlockSpec((tm, LANES), row),
            pl.BlockSpec((tm, 1), row),
            _resident((1, LANES)),
            _resident((1, LANES)),
            _resident((1, LANES)),
            _resident((1, MLA_Q_RANK)),
            _resident((1, MLA_KV_RANK)),
            _resident((MLA_Q_RANK, nq)),
            _resident((MLA_KV_RANK, nkv)),
        ],
        out_specs=[pl.BlockSpec((tm, nq), row), pl.BlockSpec((tm, nkv), row), pl.BlockSpec((tm, LANES), row)],
        out_shape=[jax.ShapeDtypeStruct((T, nq), BF16), jax.ShapeDtypeStruct((T, nkv), BF16),
                   jax.ShapeDtypeStruct((T, LANES), BF16)],
        compiler_params=_params(("parallel",)),
        name="qkv_up",
    )(lat, lat, kr, pos_col, inv, cmask, sgn, gq, gkv, w_uq_r, w_ukv_r)


def _mla_kernel(q_ref, kn_ref, kpe_ref, v_ref, o_ref, m_ref, l_ref, acc_ref, *, th, heads):
    qi = pl.program_id(2)
    m_ref[...] = jnp.full(m_ref.shape, NEG, F32)
    l_ref[...] = jnp.zeros(l_ref.shape, F32)
    acc_ref[...] = jnp.zeros(acc_ref.shape, F32)

    def load_kv(head, start, size):
        rows = pl.ds(pl.multiple_of(start, size), size)
        k = jnp.concatenate([kn_ref[rows, head * MLA_NOPE:(head + 1) * MLA_NOPE], kpe_ref[rows, :]], axis=1)
        return k, v_ref[rows, head * MLA_V:(head + 1) * MLA_V]

    def update(head, half, k, v, diag_col=None):
        chain = 2 * head + half
        q = q_ref[half * th:(half + 1) * th, head * QK_PAD:(head + 1) * QK_PAD]
        s = lax.dot_general(q, k, (((1,), (1,)), ((), ())), preferred_element_type=F32)
        chunks = [s[:, c * LANES:(c + 1) * LANES] for c in range(s.shape[1] // LANES)]
        if diag_col is not None:
            row = lax.broadcasted_iota(jnp.int32, (th, LANES), 0)
            col = lax.broadcasted_iota(jnp.int32, (th, LANES), 1)
            for c in range(diag_col // LANES, len(chunks)):
                chunks[c] = jnp.where(col + (c * LANES - diag_col) <= row, chunks[c], NEG)
        mx = chunks[0]
        for c in chunks[1:]:
            mx = jnp.maximum(mx, c)
        m_prev = m_ref[chain]
        m_new = jnp.maximum(m_prev, jnp.max(mx, axis=-1, keepdims=True))
        alpha = jnp.exp2(m_prev - m_new)
        ps = [jnp.exp2(c - m_new) for c in chunks]
        psum = ps[0]
        for p in ps[1:]:
            psum = psum + p
        l_ref[chain] = alpha * l_ref[chain] + psum
        p = jnp.concatenate(ps, axis=1).astype(BF16)
        acc_ref[chain] = alpha * acc_ref[chain] + jnp.dot(p, v, preferred_element_type=F32)
        m_ref[chain] = m_new

    def body(j, carry):
        for head in range(heads):
            k, v = load_kv(head, j * tq, tq)
            update(head, 0, k, v)
            update(head, 1, k, v)
        return carry

    tq = 2 * th
    lax.fori_loop(0, qi, body, 0)
    for head in range(heads):
        k, v = load_kv(head, qi * tq, tq)
        update(head, 0, k[:th], v[:th], diag_col=0)
        update(head, 1, k, v, diag_col=th)
    for head in range(heads):
        for half in range(2):
            l = jnp.sum(l_ref[2 * head + half], axis=-1, keepdims=True)
            o_ref[half * th:(half + 1) * th, head * MLA_V:(head + 1) * MLA_V] = (
                acc_ref[2 * head + half] / l).astype(BF16)


def _mla_attention(q, kv, kpe, B, S, th=512, heads=4):
    T = B * S
    tq = 2 * th
    nq = S // tq
    n_groups = MLA_HEADS // heads
    return pl.pallas_call(
        functools.partial(_mla_kernel, th=th, heads=heads),
        grid=(B, n_groups, nq),
        in_specs=[
            pl.BlockSpec((tq, heads * QK_PAD), lambda b, h, i: (b * nq + i, h)),
            pl.BlockSpec((S, heads * MLA_NOPE), lambda b, h, i: (b, h)),
            pl.BlockSpec((S, LANES), lambda b, h, i: (b, 0)),
            pl.BlockSpec((S, heads * MLA_V), lambda b, h, i: (b, n_groups + h)),
        ],
        out_specs=pl.BlockSpec((tq, heads * MLA_V), lambda b, h, i: (b * nq + i, h)),
        out_shape=jax.ShapeDtypeStruct((T, MIX_A), BF16),
        scratch_shapes=[
            pltpu.VMEM((2 * heads, th, LANES), F32),
            pltpu.VMEM((2 * heads, th, LANES), F32),
            pltpu.VMEM((2 * heads, th, MLA_V), F32),
        ],
        compiler_params=_params(("parallel", "parallel", "arbitrary")),
        name="mla_attn",
    )(q, kv, kpe, kv)


def _swa_kernel(slope_ref, sink_ref, q0_ref, q1_ref, q2_ref, q3_ref, k_ref, kp_ref, v_ref, vp_ref,
                pc_ref, pr_ref, prp_ref, o_ref, *, tq):
    n = pl.program_id(1)
    pair = pl.program_id(2)
    q_refs = (q0_ref, q1_ref, q2_ref, q3_ref)
    lane = lax.broadcasted_iota(jnp.int32, (2 * BLOCK, LANES), 1)
    lo = lane < SWA_HEAD_DIM
    k_all = jnp.concatenate([kp_ref[...], k_ref[...]], axis=0)
    v_all = jnp.concatenate([vp_ref[...], v_ref[...]], axis=0)
    pos_k = jnp.concatenate([prp_ref[0], pr_ref[0]], axis=1)
    i_idx = lax.broadcasted_iota(jnp.int32, (BLOCK, 2 * BLOCK), 0)
    c_idx = lax.broadcasted_iota(jnp.int32, (BLOCK, 2 * BLOCK), 1)
    in_window = (c_idx > i_idx) & (c_idx <= i_idx + SWA_WINDOW)
    zero = jnp.zeros((), BF16)
    lo_out = lax.broadcasted_iota(jnp.int32, (BLOCK, LANES), 1) < SWA_HEAD_DIM

    for t in range(tq // BLOCK):
        r0 = t * BLOCK
        band_k = k_all[r0:r0 + 2 * BLOCK]
        band_v = v_all[r0:r0 + 2 * BLOCK]
        k_bd = jnp.concatenate([jnp.where(lo, band_k, zero), jnp.where(lo, zero, band_k)], axis=0)
        v_bd = jnp.concatenate([jnp.where(lo, band_v, zero), jnp.where(lo, zero, band_v)], axis=0)
        q_st = jnp.concatenate([qr[r0:r0 + BLOCK, :] for qr in q_refs], axis=0)
        s = lax.dot_general(q_st, k_bd, (((1,), (1,)), ((), ())), preferred_element_type=F32)
        dist = jnp.abs(pc_ref[r0:r0 + BLOCK, :] - pos_k[:, r0:r0 + 2 * BLOCK]).astype(F32)
        valid = in_window
        if t == 0:
            valid = valid & ((c_idx >= BLOCK) | (n > 0))
        dist = jnp.where(valid, dist, MASKED_DIST)
        rows = []
        inv_den = []
        for g in range(SWA_GROUP):
            halves = []
            for c in range(2):
                head = (2 * pair + c) * SWA_GROUP + g
                slope = slope_ref[head] * LOG2_E
                sink = sink_ref[head] * LOG2_E
                sg = s[g * BLOCK:(g + 1) * BLOCK, c * 2 * BLOCK:(c + 1) * 2 * BLOCK] - slope * dist
                m = jnp.maximum(jnp.max(sg, axis=-1, keepdims=True), sink)
                e = jnp.exp2(sg - m)
                den = jnp.sum(e, axis=-1, keepdims=True) + jnp.exp2(sink - m)
                halves.append(e.astype(BF16))
                inv_den.append(1.0 / den)
            rows.append(jnp.concatenate(halves, axis=1))
        p = jnp.concatenate(rows, axis=0)
        out = jnp.dot(p, v_bd, preferred_element_type=F32)
        for g in range(SWA_GROUP):
            norm = jnp.where(lo_out, inv_den[2 * g], inv_den[2 * g + 1])
            o_ref[g, r0:r0 + BLOCK, :] = (out[g * BLOCK:(g + 1) * BLOCK] * norm).astype(BF16)


def _swa_attention(proj, pos_col, pos_row, slopes, sinks, B, S, tq=1024):
    T = B * S
    nt = S // tq
    sub = tq // BLOCK
    npairs = SWA_KV_HEADS // 2
    qs0 = COL_QS // LANES
    ks0 = COL_KS // LANES
    vs0 = COL_VS // LANES
    pairs_per_group = KV_W // LANES

    def q_spec(g):
        return pl.BlockSpec((tq, LANES), lambda b, n, j, g=g: (b * nt + n, qs0 + pairs_per_group * g + j))

    def prev_row(b, n):
        return jnp.maximum(b * (S // BLOCK) + sub * n - 1, 0)

    smem = pl.BlockSpec(memory_space=pltpu.SMEM)
    return pl.pallas_call(
        functools.partial(_swa_kernel, tq=tq),
        grid=(B, nt, npairs),
        in_specs=[
            smem, smem,
            q_spec(0), q_spec(1), q_spec(2), q_spec(3),
            pl.BlockSpec((tq, LANES), lambda b, n, j: (b * nt + n, ks0 + j)),
            pl.BlockSpec((BLOCK, LANES), lambda b, n, j: (prev_row(b, n), ks0 + j)),
            pl.BlockSpec((tq, LANES), lambda b, n, j: (b * nt + n, vs0 + j)),
            pl.BlockSpec((BLOCK, LANES), lambda b, n, j: (prev_row(b, n), vs0 + j)),
            pl.BlockSpec((tq, 1), lambda b, n, j: (b * nt + n, 0)),
            pl.BlockSpec((1, 1, tq), lambda b, n, j: (b, 0, n)),
            pl.BlockSpec((1, 1, BLOCK), lambda b, n, j: (b, 0, jnp.maximum(sub * n - 1, 0))),
        ],
        out_specs=pl.BlockSpec((SWA_GROUP, tq, LANES), lambda b, n, j: (0, b * nt + n, j)),
        out_shape=jax.ShapeDtypeStruct((SWA_GROUP, T, KV_W), BF16),
        compiler_params=_params(("parallel", "parallel", "arbitrary")),
        name="swa_attn",
    )(slopes, sinks, proj, proj, proj, proj, proj, proj, proj, proj, pos_col, pos_row, pos_row)


def _oproj_a_kernel(oa_ref, ga_ref, w_ref, y_ref):
    oa = oa_ref[...].astype(F32)
    r = _row_rsqrt(oa)
    og = (oa * ga_ref[...]).astype(BF16)
    for cs in _col_chunks(D_MODEL):
        y_ref[:, cs] = jnp.dot(og, w_ref[:, cs], preferred_element_type=F32) * r


def _out_proj_a(o_a, ga, w_oa, tm=512):
    T = o_a.shape[0]
    row = lambda i: (i, 0)
    return pl.pallas_call(
        _oproj_a_kernel,
        grid=(T // tm,),
        in_specs=[pl.BlockSpec((tm, MIX_A), row), _resident((1, MIX_A)), _resident((MIX_A, D_MODEL))],
        out_specs=pl.BlockSpec((tm, D_MODEL), row),
        out_shape=jax.ShapeDtypeStruct((T, D_MODEL), F32),
        compiler_params=_params(("parallel",)),
        name="out_proj_a",
    )(o_a, ga, w_oa)


def _oproj_b_kernel(ob_ref, gb_ref, w_ref, ya_ref, x_ref, gp_ref, h_ref):
    ob = jnp.concatenate([ob_ref[g] for g in range(SWA_GROUP)], axis=1).astype(F32)
    r = _row_rsqrt(ob)
    og = (ob * gb_ref[...]).astype(BF16)
    ssq = jnp.zeros((h_ref.shape[0], 1), F32)
    for cs in _col_chunks(D_MODEL):
        y = ya_ref[:, cs] + jnp.dot(og, w_ref[:, cs], preferred_element_type=F32) * r
        ssq = ssq + jnp.sum(y * y, axis=-1, keepdims=True)
        h_ref[:, cs] = y
    rn = lax.rsqrt(ssq * (1.0 / D_MODEL) + EPS)
    for cs in _col_chunks(D_MODEL):
        h_ref[:, cs] = x_ref[:, cs] + h_ref[:, cs] * rn * gp_ref[:, cs]


def _out_proj_b(o_b, gb, w_ob, y_a, x2, gpost, tm=256):
    T = x2.shape[0]
    row = lambda i: (i, 0)
    return pl.pallas_call(
        _oproj_b_kernel,
        grid=(T // tm,),
        in_specs=[
            pl.BlockSpec((SWA_GROUP, tm, KV_W), lambda i: (0, i, 0)),
            _resident((1, MIX_B)),
            _resident((MIX_B, D_MODEL)),
            pl.BlockSpec((tm, D_MODEL), row),
            pl.BlockSpec((tm, D_MODEL), row),
            _resident((1, D_MODEL)),
        ],
        out_specs=pl.BlockSpec((tm, D_MODEL), row),
        out_shape=jax.ShapeDtypeStruct((T, D_MODEL), F32),
        compiler_params=_params(("parallel",)),
        name="out_proj_b",
    )(o_b, gb, w_ob, y_a, x2, gpost)


def _ffn_kernel(h_ref, gpre_ref, wg_ref, wu_ref, wd_ref, gpost_ref, o_ref, f_ref, r_ref, *, nj):
    j = pl.program_id(1)

    @pl.when(j == 0)
    def _():
        def prep_rows(rows):
            x = h_ref[rows, :]
            f_ref[rows, :] = (x * gpre_ref[...]).astype(BF16)
            r_ref[rows, :] = _row_rsqrt(x)
            o_ref[rows, :] = jnp.zeros((PREP_ROWS, D_MODEL), F32)
        _for_row_chunks(f_ref.shape[0], prep_rows, PREP_ROWS, unroll=PREP_UNROLL)

    f = f_ref[...]
    r = r_ref[...]
    gate = jnp.dot(f, wg_ref[...], preferred_element_type=F32) * r
    up = jnp.dot(f, wu_ref[...], preferred_element_type=F32) * r
    act = (gate * jax.nn.sigmoid(gate) * up).astype(BF16)
    for cs in _col_chunks(D_MODEL, DOWN_CHUNK):
        o_ref[:, cs] += jnp.dot(act, wd_ref[:, cs], preferred_element_type=F32)

    @pl.when(j == nj - 1)
    def _():
        def finish_rows(rows):
            o_ref[rows, :] = h_ref[rows, :] + _rms(o_ref[rows, :], gpost_ref[...])
        _for_row_chunks(o_ref.shape[0], finish_rows, FINISH_ROWS)


def _ffn(h, gpre, wg, wu, wd, gpost, tm=512, tf=FF_TILE):
    T = h.shape[0]
    nj = D_FF // tf
    return pl.pallas_call(
        functools.partial(_ffn_kernel, nj=nj),
        grid=(T // tm, nj),
        in_specs=[
            pl.BlockSpec((tm, D_MODEL), lambda i, j: (i, 0)),
            pl.BlockSpec((1, D_MODEL), lambda i, j: (0, 0)),
            pl.BlockSpec((D_MODEL, tf), lambda i, j: (0, j)),
            pl.BlockSpec((D_MODEL, tf), lambda i, j: (0, j)),
            pl.BlockSpec((tf, D_MODEL), lambda i, j: (j, 0)),
            pl.BlockSpec((1, D_MODEL), lambda i, j: (0, 0)),
        ],
        out_specs=pl.BlockSpec((tm, D_MODEL), lambda i, j: (i, 0)),
        out_shape=jax.ShapeDtypeStruct((T, D_MODEL), F32),
        scratch_shapes=[pltpu.VMEM((tm, D_MODEL), BF16), pltpu.VMEM((tm, 1), F32)],
        compiler_params=_params(("parallel", "arbitrary")),
        name="ffn",
    )(h, gpre, wg, wu, wd, gpost)


def _swap_halves(w):
    half = w.shape[-1] // 2
    return jnp.concatenate([w[..., half:], w[..., :half]], axis=-1)


def _prep_w_in(w_in):
    c0 = LATENT_W
    c1 = c0 + MLA_ROPE
    c2 = c1 + MIX_B
    w_lat = w_in[:, :c0].astype(BF16)
    kr = w_in[:, c0:c1].astype(BF16)
    w_rope = jnp.concatenate([kr, _swap_halves(kr)], axis=1)
    qs = w_in[:, c1:c2].astype(BF16).reshape(D_MODEL, SWA_KV_HEADS, SWA_GROUP, SWA_HEAD_DIM)
    w_q = qs.transpose(0, 2, 1, 3).reshape(D_MODEL, MIX_B)
    w_kv = w_in[:, c2:].astype(BF16)
    return w_lat, w_rope, w_q, w_kv


def _prep_w_uq(w_uq):
    w = w_uq.astype(BF16).reshape(MLA_Q_RANK, MLA_HEADS, MLA_QK)
    pe = w[..., MLA_NOPE:]
    w = jnp.concatenate([w[..., :MLA_NOPE], pe, _swap_halves(pe)], axis=-1)
    return w.reshape(MLA_Q_RANK, MLA_HEADS * QK_PAD)


def _prep_w_ukv(w_ukv):
    w = w_ukv.astype(BF16).reshape(MLA_KV_RANK, MLA_HEADS, MLA_NOPE + MLA_V)
    kn = w[..., :MLA_NOPE].reshape(MLA_KV_RANK, MLA_HEADS * MLA_NOPE)
    v = w[..., MLA_NOPE:].reshape(MLA_KV_RANK, MLA_HEADS * MLA_V)
    return jnp.concatenate([kn, v], axis=1)


def _swa_perm_rows(a):
    rest = a.shape[1:]
    a = a.reshape((SWA_KV_HEADS, SWA_GROUP, SWA_HEAD_DIM) + rest)
    return jnp.swapaxes(a, 0, 1).reshape((MIX_B,) + rest)


def kernel(x, positions, attn_pre_g, w_in, q_norm_g, w_uq, kv_norm_g, w_ukv, swa_sinks, grp_a_g, grp_b_g,
           w_o, attn_post_g, ffn_pre_g, w_gate, w_up, w_down, ffn_post_g):
    B, S, _ = x.shape
    T = B * S
    depth = w_in.shape[0]
    inv = 1.0 / (ROPE_THETA ** (jnp.arange(0, MLA_ROPE, 2, dtype=F32) / MLA_ROPE))
    zeros64 = jnp.zeros((MLA_ROPE,), F32)
    inv_l = jnp.concatenate([inv, inv, zeros64]).reshape(1, LANES)
    cmask = jnp.concatenate([jnp.ones((MLA_ROPE,), F32), zeros64]).reshape(1, LANES)
    half = MLA_ROPE // 2
    sgn = jnp.concatenate([-jnp.ones((half,), F32), jnp.ones((half,), F32), zeros64]).reshape(1, LANES)
    slopes = jnp.exp2(-8.0 * jnp.arange(1, SWA_Q_HEADS + 1, dtype=F32) / SWA_Q_HEADS)
    pos_col = positions.reshape(T, 1)
    pos_row = positions.reshape(B, 1, S)

    h = x.reshape(T, D_MODEL)
    for l in range(depth):
        w_lat, w_rope, w_q, w_kv = _prep_w_in(w_in[l])
        w_oa = w_o[l][:MIX_A].astype(BF16)
        w_ob = _swa_perm_rows(w_o[l][MIX_A:].astype(BF16))
        gb = _swa_perm_rows(grp_b_g[l]).reshape(1, MIX_B)
        g_pre = attn_pre_g[l].reshape(1, D_MODEL)

        lat, kr = _in_proj_mla(h, g_pre, w_lat, w_rope)
        proj_swa = _in_proj_swa(h, g_pre, w_q, w_kv)
        q, kv, kpe = _qkv_up(lat, kr, pos_col, inv_l, cmask, sgn, q_norm_g[l].reshape(1, MLA_Q_RANK),
                             kv_norm_g[l].reshape(1, MLA_KV_RANK), _prep_w_uq(w_uq[l]), _prep_w_ukv(w_ukv[l]))
        o_a = _mla_attention(q, kv, kpe, B, S)
        y_a = _out_proj_a(o_a, grp_a_g[l].reshape(1, MIX_A), w_oa)
        o_b = _swa_attention(proj_swa, pos_col, pos_row, slopes, swa_sinks[l].astype(F32), B, S)
        h = _out_proj_b(o_b, gb, w_ob, y_a, h, attn_post_g[l].reshape(1, D_MODEL))
        h = _ffn(h, ffn_pre_g[l].reshape(1, D_MODEL), w_gate[l].astype(BF16), w_up[l].astype(BF16),
                 w_down[l].astype(BF16), ffn_post_g[l].reshape(1, D_MODEL))
    return h.reshape(B, S, D_MODEL)
```

```python
import functools

import jax
import jax.numpy as jnp
from jax import lax
from jax.experimental import pallas as pl
from jax.experimental.pallas import tpu as pltpu

D_MODEL = 4096
MLA_HEADS = 16
MLA_Q_RANK = 1024
MLA_KV_RANK = 512
MLA_NOPE = 128
MLA_ROPE = 64
MLA_V = 128
MLA_QK = MLA_NOPE + MLA_ROPE
ROPE_THETA = 10000.0
SWA_Q_HEADS = 32
SWA_KV_HEADS = 8
SWA_HEAD_DIM = 64
SWA_GROUP = SWA_Q_HEADS // SWA_KV_HEADS
SWA_WINDOW = 128
BLOCK = 128
MIX_A = MLA_HEADS * MLA_V
MIX_B = SWA_Q_HEADS * SWA_HEAD_DIM
D_FF = 11008
EPS = 1e-6

LANES = 128
QK_PAD = 256
KV_W = SWA_KV_HEADS * SWA_HEAD_DIM
LATENT_W = MLA_Q_RANK + MLA_KV_RANK
COL_QS = 0
COL_KS = COL_QS + MIX_B
COL_VS = COL_KS + KV_W
SWA_IN_W = COL_VS + KV_W
NEG = float(jnp.finfo(jnp.float32).min)
VMEM_LIMIT = 56 * 1024 * 1024
DOWN_CHUNK = 1024
FF_TILE = 256
N_CHUNK = 1024
FINISH_ROWS = 64
LOG2_E = 1.4426950408889634
MASKED_DIST = 1e30
F32 = jnp.float32
BF16 = jnp.bfloat16


def _rms(xf, g):
    return xf * lax.rsqrt(jnp.mean(xf * xf, axis=-1, keepdims=True) + EPS) * g


def _row_rsqrt(xf):
    return lax.rsqrt(jnp.mean(xf * xf, axis=-1, keepdims=True) + EPS)


def _for_row_chunks(n_rows, fn, chunk, unroll=1):
    def body(r, carry):
        fn(pl.ds(pl.multiple_of(r * chunk, chunk), chunk))
        return carry
    lax.fori_loop(0, n_rows // chunk, body, 0, unroll=unroll)


def _params(sem):
    return pltpu.CompilerParams(dimension_semantics=sem, vmem_limit_bytes=VMEM_LIMIT)


def _resident(shape):
    return pl.BlockSpec(shape, lambda *_: (0,) * len(shape), pipeline_mode=pl.Buffered(1))


def _col_chunks(width, chunk=N_CHUNK):
    return [slice(lo, min(lo + chunk, width)) for lo in range(0, width, chunk)]


def _inproj_mla_kernel(x_ref, g_ref, wl_ref, wr_ref, lat_ref, kr_ref):
    x = x_ref[...]
    r = _row_rsqrt(x)
    xg = (x * g_ref[...]).astype(BF16)
    lat_ref[...] = (jnp.dot(xg, wl_ref[...], preferred_element_type=F32) * r).astype(BF16)
    kr_ref[...] = jnp.dot(xg, wr_ref[...], preferred_element_type=F32) * r


def _in_proj_mla(x2, g, w_lat, w_rope, tm=512):
    T = x2.shape[0]
    row = lambda i: (i, 0)
    return pl.pallas_call(
        _inproj_mla_kernel,
        grid=(T // tm,),
        in_specs=[
            pl.BlockSpec((tm, D_MODEL), row),
            _resident((1, D_MODEL)),
            _resident((D_MODEL, LATENT_W)),
            _resident((D_MODEL, LANES)),
        ],
        out_specs=[pl.BlockSpec((tm, LATENT_W), row), pl.BlockSpec((tm, LANES), row)],
        out_shape=[jax.ShapeDtypeStruct((T, LATENT_W), BF16), jax.ShapeDtypeStruct((T, LANES), F32)],
        compiler_params=_params(("parallel",)),
        name="in_proj_mla",
    )(x2, g, w_lat, w_rope)


def _inproj_swa_kernel(x_ref, g_ref, wq_ref, wkv_ref, o_ref):
    x = x_ref[...]
    r = _row_rsqrt(x)
    xg = (x * g_ref[...]).astype(BF16)
    rq = r * (SWA_HEAD_DIM ** -0.5 * LOG2_E)
    for cs in _col_chunks(MIX_B):
        o_ref[:, cs] = (jnp.dot(xg, wq_ref[:, cs], preferred_element_type=F32) * rq).astype(BF16)
    o_ref[:, COL_KS:] = (jnp.dot(xg, wkv_ref[...], preferred_element_type=F32) * r).astype(BF16)


def _in_proj_swa(x2, g, w_q, w_kv, tm=256):
    T = x2.shape[0]
    row = lambda i: (i, 0)
    return pl.pallas_call(
        _inproj_swa_kernel,
        grid=(T // tm,),
        in_specs=[
            pl.BlockSpec((tm, D_MODEL), row),
            _resident((1, D_MODEL)),
            _resident((D_MODEL, MIX_B)),
            _resident((D_MODEL, 2 * KV_W)),
        ],
        out_specs=pl.BlockSpec((tm, SWA_IN_W), row),
        out_shape=jax.ShapeDtypeStruct((T, SWA_IN_W), BF16),
        compiler_params=_params(("parallel",)),
        name="in_proj_swa",
    )(x2, g, w_q, w_kv)


def _qkvup_kernel(cq_ref, ckv_ref, kr_ref, pos_ref, inv_ref, cmask_ref, sgn_ref, gq_ref, gkv_ref,
                  wq_ref, wkv_ref, q_ref, kv_ref, kpe_ref):
    ang = pos_ref[...].astype(F32) * inv_ref[...]
    c = jnp.cos(ang) * cmask_ref[...]
    s = jnp.sin(ang) * sgn_ref[...]
    kr = kr_ref[...]
    kpe_ref[...] = (kr * c + pltpu.roll(kr, 64, 1) * s).astype(BF16)
    scale = MLA_QK ** -0.5 * LOG2_E
    cq = cq_ref[...].astype(F32)
    rq = _row_rsqrt(cq) * scale
    cqg = (cq * gq_ref[...]).astype(BF16)
    for cs in _col_chunks(MLA_HEADS * QK_PAD):
        y = jnp.dot(cqg, wq_ref[:, cs], preferred_element_type=F32) * rq
        for h in range((cs.stop - cs.start) // QK_PAD):
            lo = h * QK_PAD
            q_ref[:, cs.start + lo:cs.start + lo + MLA_NOPE] = y[:, lo:lo + MLA_NOPE].astype(BF16)
            pe = y[:, lo + MLA_NOPE:lo + QK_PAD]
            q_ref[:, cs.start + lo + MLA_NOPE:cs.start + lo + QK_PAD] = (
                pe * c + pltpu.roll(pe, 64, 1) * s).astype(BF16)
    ckv = ckv_ref[...].astype(F32)
    rkv = _row_rsqrt(ckv)
    ckvg = (ckv * gkv_ref[...]).astype(BF16)
    for cs in _col_chunks(MLA_HEADS * (MLA_NOPE + MLA_V)):
        kv_ref[:, cs] = (jnp.dot(ckvg, wkv_ref[:, cs], preferred_element_type=F32) * rkv).astype(BF16)


def _qkv_up(lat, kr, pos_col, inv, cmask, sgn, gq, gkv, w_uq_r, w_ukv_r, tm=512):
    T = lat.shape[0]
    nq = MLA_HEADS * QK_PAD
    nkv = MLA_HEADS * (MLA_NOPE + MLA_V)
    row = lambda i: (i, 0)
    return pl.pallas_call(
        _qkvup_kernel,
        grid=(T // tm,),
        in_specs=[
            pl.BlockSpec((tm, MLA_Q_RANK), row),
            pl.BlockSpec((tm, MLA_KV_RANK), lambda i: (i, MLA_Q_RANK // MLA_KV_RANK)),
            pl.BlockSpec((tm, LANES), row),
            pl.BlockSpec((tm, 1), row),
            _resident((1, LANES)),
            _resident((1, LANES)),
            _resident((1, LANES)),
            _resident((1, MLA_Q_RANK)),
            _resident((1, MLA_KV_RANK)),
            _resident((MLA_Q_RANK, nq)),
            _resident((MLA_KV_RANK, nkv)),
        ],
        out_specs=[pl.BlockSpec((tm, nq), row), pl.BlockSpec((tm, nkv), row), pl.BlockSpec((tm, LANES), row)],
        out_shape=[jax.ShapeDtypeStruct((T, nq), BF16), jax.ShapeDtypeStruct((T, nkv), BF16),
                   jax.ShapeDtypeStruct((T, LANES), BF16)],
        compiler_params=_params(("parallel",)),
        name="qkv_up",
    )(lat, lat, kr, pos_col, inv, cmask, sgn, gq, gkv, w_uq_r, w_ukv_r)


def _mla_kernel(q_ref, kn_ref, kpe_ref, v_ref, wg_ref, wu_ref, o_ref, wg_bf_ref, wu_bf_ref,
                m_ref, l_ref, acc_ref, *, th, heads):
    wg_bf_ref[...] = wg_ref[...].astype(BF16)
    wu_bf_ref[...] = wu_ref[...].astype(BF16)
    qi = pl.program_id(2)
    m_ref[...] = jnp.full(m_ref.shape, NEG, F32)
    l_ref[...] = jnp.zeros(l_ref.shape, F32)
    acc_ref[...] = jnp.zeros(acc_ref.shape, F32)

    def load_kv(head, start, size):
        rows = pl.ds(pl.multiple_of(start, size), size)
        k = jnp.concatenate([kn_ref[rows, head * MLA_NOPE:(head + 1) * MLA_NOPE], kpe_ref[rows, :]], axis=1)
        return k, v_ref[rows, head * MLA_V:(head + 1) * MLA_V]

    def update(head, half, k, v, diag_col=None):
        chain = 2 * head + half
        q = q_ref[half * th:(half + 1) * th, head * QK_PAD:(head + 1) * QK_PAD]
        s = lax.dot_general(q, k, (((1,), (1,)), ((), ())), preferred_element_type=F32)
        chunks = [s[:, c * LANES:(c + 1) * LANES] for c in range(s.shape[1] // LANES)]
        if diag_col is not None:
            row = lax.broadcasted_iota(jnp.int32, (th, LANES), 0)
            col = lax.broadcasted_iota(jnp.int32, (th, LANES), 1)
            for c in range(diag_col // LANES, len(chunks)):
                chunks[c] = jnp.where(col + (c * LANES - diag_col) <= row, chunks[c], NEG)
        mx = chunks[0]
        for c in chunks[1:]:
            mx = jnp.maximum(mx, c)
        m_prev = m_ref[chain]
        m_new = jnp.maximum(m_prev, jnp.max(mx, axis=-1, keepdims=True))
        alpha = jnp.exp2(m_prev - m_new)
        ps = [jnp.exp2(c - m_new) for c in chunks]
        psum = ps[0]
        for p in ps[1:]:
            psum = psum + p
        l_ref[chain] = alpha * l_ref[chain] + psum
        p = jnp.concatenate(ps, axis=1).astype(BF16)
        acc_ref[chain] = alpha * acc_ref[chain] + jnp.dot(p, v, preferred_element_type=F32)
        m_ref[chain] = m_new

    def body(j, carry):
        for head in range(heads):
            k, v = load_kv(head, j * tq, tq)
            update(head, 0, k, v)
            update(head, 1, k, v)
        return carry

    tq = 2 * th
    lax.fori_loop(0, qi, body, 0)
    for head in range(heads):
        k, v = load_kv(head, qi * tq, tq)
        update(head, 0, k[:th], v[:th], diag_col=0)
        update(head, 1, k, v, diag_col=th)
    for head in range(heads):
        for half in range(2):
            l = jnp.sum(l_ref[2 * head + half], axis=-1, keepdims=True)
            o_ref[half * th:(half + 1) * th, head * MLA_V:(head + 1) * MLA_V] = (
                acc_ref[2 * head + half] / l).astype(BF16)


def _mla_attention(q, kv, kpe, w_gate, w_up, B, S, th=512, heads=4):
    T = B * S
    tq = 2 * th
    nq = S // tq
    n_groups = MLA_HEADS // heads
    slab = D_MODEL // (B * n_groups * nq)
    w_spec = pl.BlockSpec((slab, D_FF), lambda b, h, i: ((b * n_groups + h) * nq + i, 0))
    return pl.pallas_call(
        functools.partial(_mla_kernel, th=th, heads=heads),
        grid=(B, n_groups, nq),
        in_specs=[
            pl.BlockSpec((tq, heads * QK_PAD), lambda b, h, i: (b * nq + i, h)),
            pl.BlockSpec((S, heads * MLA_NOPE), lambda b, h, i: (b, h)),
            pl.BlockSpec((S, LANES), lambda b, h, i: (b, 0)),
            pl.BlockSpec((S, heads * MLA_V), lambda b, h, i: (b, n_groups + h)),
            w_spec, w_spec,
        ],
        out_specs=[pl.BlockSpec((tq, heads * MLA_V), lambda b, h, i: (b * nq + i, h)), w_spec, w_spec],
        out_shape=[jax.ShapeDtypeStruct((T, MIX_A), BF16),
                   jax.ShapeDtypeStruct((D_MODEL, D_FF), BF16),
                   jax.ShapeDtypeStruct((D_MODEL, D_FF), BF16)],
        scratch_shapes=[
            pltpu.VMEM((2 * heads, th, LANES), F32),
            pltpu.VMEM((2 * heads, th, LANES), F32),
            pltpu.VMEM((2 * heads, th, MLA_V), F32),
        ],
        compiler_params=_params(("arbitrary", "arbitrary", "arbitrary")),
        name="mla_attn",
    )(q, kv, kpe, kv, w_gate, w_up)


def _swa_kernel(slope_ref, sink_ref, q0_ref, q1_ref, q2_ref, q3_ref, k_ref, kp_ref, v_ref, vp_ref,
                pc_ref, pr_ref, prp_ref, wd_ref, o_ref, wd_bf_ref, *, tq):
    wd_bf_ref[...] = wd_ref[...].astype(BF16)
    n = pl.program_id(1)
    pair = pl.program_id(2)
    q_refs = (q0_ref, q1_ref, q2_ref, q3_ref)
    lane = lax.broadcasted_iota(jnp.int32, (2 * BLOCK, LANES), 1)
    lo = lane < SWA_HEAD_DIM
    k_all = jnp.concatenate([kp_ref[...], k_ref[...]], axis=0)
    v_all = jnp.concatenate([vp_ref[...], v_ref[...]], axis=0)
    pos_k = jnp.concatenate([prp_ref[0], pr_ref[0]], axis=1)
    i_idx = lax.broadcasted_iota(jnp.int32, (BLOCK, 2 * BLOCK), 0)
    c_idx = lax.broadcasted_iota(jnp.int32, (BLOCK, 2 * BLOCK), 1)
    in_window = (c_idx > i_idx) & (c_idx <= i_idx + SWA_WINDOW)
    zero = jnp.zeros((), BF16)
    lo_out = lax.broadcasted_iota(jnp.int32, (BLOCK, LANES), 1) < SWA_HEAD_DIM

    for t in range(tq // BLOCK):
        r0 = t * BLOCK
        band_k = k_all[r0:r0 + 2 * BLOCK]
        band_v = v_all[r0:r0 + 2 * BLOCK]
        k_bd = jnp.concatenate([jnp.where(lo, band_k, zero), jnp.where(lo, zero, band_k)], axis=0)
        v_bd = jnp.concatenate([jnp.where(lo, band_v, zero), jnp.where(lo, zero, band_v)], axis=0)
        q_st = jnp.concatenate([qr[r0:r0 + BLOCK, :] for qr in q_refs], axis=0)
        s = lax.dot_general(q_st, k_bd, (((1,), (1,)), ((), ())), preferred_element_type=F32)
        dist = jnp.abs(pc_ref[r0:r0 + BLOCK, :] - pos_k[:, r0:r0 + 2 * BLOCK]).astype(F32)
        valid = in_window
        if t == 0:
            valid = valid & ((c_idx >= BLOCK) | (n > 0))
        dist = jnp.where(valid, dist, MASKED_DIST)
        rows = []
        inv_den = []
        for g in range(SWA_GROUP):
            halves = []
            for c in range(2):
                head = (2 * pair + c) * SWA_GROUP + g
                slope = slope_ref[head] * LOG2_E
                sink = sink_ref[head] * LOG2_E
                sg = s[g * BLOCK:(g + 1) * BLOCK, c * 2 * BLOCK:(c + 1) * 2 * BLOCK] - slope * dist
                m = jnp.maximum(jnp.max(sg, axis=-1, keepdims=True), sink)
                e = jnp.exp2(sg - m)
                den = jnp.sum(e, axis=-1, keepdims=True) + jnp.exp2(sink - m)
                halves.append(e.astype(BF16))
                inv_den.append(1.0 / den)
            rows.append(jnp.concatenate(halves, axis=1))
        p = jnp.concatenate(rows, axis=0)
        out = jnp.dot(p, v_bd, preferred_element_type=F32)
        for g in range(SWA_GROUP):
            norm = jnp.where(lo_out, inv_den[2 * g], inv_den[2 * g + 1])
            o_ref[g, r0:r0 + BLOCK, :] = (out[g * BLOCK:(g + 1) * BLOCK] * norm).astype(BF16)


def _swa_attention(proj, pos_col, pos_row, slopes, sinks, w_down, B, S, tq=1024):
    T = B * S
    nt = S // tq
    sub = tq // BLOCK
    npairs = SWA_KV_HEADS // 2
    n_steps = B * nt * npairs
    slab = FF_TILE if n_steps >= D_FF // FF_TILE else D_FF
    last_slab = D_FF // slab - 1
    wd_spec = pl.BlockSpec(
        (slab, D_MODEL), lambda b, n, j: (jnp.minimum((b * nt + n) * npairs + j, last_slab), 0))
    qs0 = COL_QS // LANES
    ks0 = COL_KS // LANES
    vs0 = COL_VS // LANES
    pairs_per_group = KV_W // LANES

    def q_spec(g):
        return pl.BlockSpec((tq, LANES), lambda b, n, j, g=g: (b * nt + n, qs0 + pairs_per_group * g + j))

    def prev_row(b, n):
        return jnp.maximum(b * (S // BLOCK) + sub * n - 1, 0)

    smem = pl.BlockSpec(memory_space=pltpu.SMEM)
    return pl.pallas_call(
        functools.partial(_swa_kernel, tq=tq),
        grid=(B, nt, npairs),
        in_specs=[
            smem, smem,
            q_spec(0), q_spec(1), q_spec(2), q_spec(3),
            pl.BlockSpec((tq, LANES), lambda b, n, j: (b * nt + n, ks0 + j)),
            pl.BlockSpec((BLOCK, LANES), lambda b, n, j: (prev_row(b, n), ks0 + j)),
            pl.BlockSpec((tq, LANES), lambda b, n, j: (b * nt + n, vs0 + j)),
            pl.BlockSpec((BLOCK, LANES), lambda b, n, j: (prev_row(b, n), vs0 + j)),
            pl.BlockSpec((tq, 1), lambda b, n, j: (b * nt + n, 0)),
            pl.BlockSpec((1, 1, tq), lambda b, n, j: (b, 0, n)),
            pl.BlockSpec((1, 1, BLOCK), lambda b, n, j: (b, 0, jnp.maximum(sub * n - 1, 0))),
            wd_spec,
        ],
        out_specs=[pl.BlockSpec((SWA_GROUP, tq, LANES), lambda b, n, j: (0, b * nt + n, j)), wd_spec],
        out_shape=[jax.ShapeDtypeStruct((SWA_GROUP, T, KV_W), BF16),
                   jax.ShapeDtypeStruct((D_FF, D_MODEL), BF16)],
        compiler_params=_params(("arbitrary", "arbitrary", "arbitrary")),
        name="swa_attn",
    )(slopes, sinks, proj, proj, proj, proj, proj, proj, proj, proj, pos_col, pos_row, pos_row, w_down)


def _oproj_a_kernel(oa_ref, ga_ref, w_ref, y_ref):
    oa = oa_ref[...].astype(F32)
    r = _row_rsqrt(oa)
    og = (oa * ga_ref[...]).astype(BF16)
    for cs in _col_chunks(D_MODEL):
        y_ref[:, cs] = jnp.dot(og, w_ref[:, cs], preferred_element_type=F32) * r


def _out_proj_a(o_a, ga, w_oa, tm=512):
    T = o_a.shape[0]
    row = lambda i: (i, 0)
    return pl.pallas_call(
        _oproj_a_kernel,
        grid=(T // tm,),
        in_specs=[pl.BlockSpec((tm, MIX_A), row), _resident((1, MIX_A)), _resident((MIX_A, D_MODEL))],
        out_specs=pl.BlockSpec((tm, D_MODEL), row),
        out_shape=jax.ShapeDtypeStruct((T, D_MODEL), F32),
        compiler_params=_params(("parallel",)),
        name="out_proj_a",
    )(o_a, ga, w_oa)


def _oproj_b_kernel(ob_ref, gb_ref, w_ref, ya_ref, x_ref, gp_ref, h_ref):
    ob = jnp.concatenate([ob_ref[g] for g in range(SWA_GROUP)], axis=1).astype(F32)
    r = _row_rsqrt(ob)
    og = (ob * gb_ref[...]).astype(BF16)
    ssq = jnp.zeros((h_ref.shape[0], 1), F32)
    for cs in _col_chunks(D_MODEL):
        y = ya_ref[:, cs] + jnp.dot(og, w_ref[:, cs], preferred_element_type=F32) * r
        ssq = ssq + jnp.sum(y * y, axis=-1, keepdims=True)
        h_ref[:, cs] = y
    rn = lax.rsqrt(ssq * (1.0 / D_MODEL) + EPS)
    for cs in _col_chunks(D_MODEL):
        h_ref[:, cs] = x_ref[:, cs] + h_ref[:, cs] * rn * gp_ref[:, cs]


def _out_proj_b(o_b, gb, w_ob, y_a, x2, gpost, tm=256):
    T = x2.shape[0]
    row = lambda i: (i, 0)
    return pl.pallas_call(
        _oproj_b_kernel,
        grid=(T // tm,),
        in_specs=[
            pl.BlockSpec((SWA_GROUP, tm, KV_W), lambda i: (0, i, 0)),
            _resident((1, MIX_B)),
            _resident((MIX_B, D_MODEL)),
            pl.BlockSpec((tm, D_MODEL), row),
            pl.BlockSpec((tm, D_MODEL), row),
            _resident((1, D_MODEL)),
        ],
        out_specs=pl.BlockSpec((tm, D_MODEL), row),
        out_shape=jax.ShapeDtypeStruct((T, D_MODEL), F32),
        compiler_params=_params(("parallel",)),
        name="out_proj_b",
    )(o_b, gb, w_ob, y_a, x2, gpost)


def _ffn_kernel(h_ref, gpre_ref, wg_ref, wu_ref, wd_ref, gpost_ref, o_ref, f_ref, *, nj):
    j = pl.program_id(1)

    @pl.when(j == 0)
    def _():
        def norm_rows(rows):
            f_ref[rows, :] = _rms(h_ref[rows, :], gpre_ref[...]).astype(BF16)
            o_ref[rows, :] = jnp.zeros((FINISH_ROWS, D_MODEL), F32)
        _for_row_chunks(f_ref.shape[0], norm_rows, FINISH_ROWS)

    f = f_ref[...]
    gate = jnp.dot(f, wg_ref[...], preferred_element_type=F32)
    up = jnp.dot(f, wu_ref[...], preferred_element_type=F32)
    act = (gate * jax.nn.sigmoid(gate) * up).astype(BF16)
    for cs in _col_chunks(D_MODEL, DOWN_CHUNK):
        o_ref[:, cs] += jnp.dot(act, wd_ref[:, cs], preferred_element_type=F32)

    @pl.when(j == nj - 1)
    def _():
        def finish_rows(rows):
            o_ref[rows, :] = h_ref[rows, :] + _rms(o_ref[rows, :], gpost_ref[...])
        _for_row_chunks(o_ref.shape[0], finish_rows, FINISH_ROWS)


def _ffn(h, gpre, wg, wu, wd, gpost, tm=512, tf=FF_TILE):
    T = h.shape[0]
    nj = D_FF // tf
    return pl.pallas_call(
        functools.partial(_ffn_kernel, nj=nj),
        grid=(T // tm, nj),
        in_specs=[
            pl.BlockSpec((tm, D_MODEL), lambda i, j: (i, 0)),
            pl.BlockSpec((1, D_MODEL), lambda i, j: (0, 0)),
            pl.BlockSpec((D_MODEL, tf), lambda i, j: (0, j)),
            pl.BlockSpec((D_MODEL, tf), lambda i, j: (0, j)),
            pl.BlockSpec((tf, D_MODEL), lambda i, j: (j, 0)),
            pl.BlockSpec((1, D_MODEL), lambda i, j: (0, 0)),
        ],
        out_specs=pl.BlockSpec((tm, D_MODEL), lambda i, j: (i, 0)),
        out_shape=jax.ShapeDtypeStruct((T, D_MODEL), F32),
        scratch_shapes=[pltpu.VMEM((tm, D_MODEL), BF16)],
        compiler_params=_params(("parallel", "arbitrary")),
        name="ffn",
    )(h, gpre, wg, wu, wd, gpost)


def _swap_halves(w):
    half = w.shape[-1] // 2
    return jnp.concatenate([w[..., half:], w[..., :half]], axis=-1)


def _prep_w_in(w_in):
    c0 = LATENT_W
    c1 = c0 + MLA_ROPE
    c2 = c1 + MIX_B
    w_in = w_in.astype(BF16)
    kr = w_in[:, c0:c1]
    w_rope = jnp.concatenate([kr, _swap_halves(kr)], axis=1)
    qs = w_in[:, c1:c2].reshape(D_MODEL, SWA_KV_HEADS, SWA_GROUP, SWA_HEAD_DIM)
    w_q = qs.transpose(0, 2, 1, 3).reshape(D_MODEL, MIX_B)
    w_kv = w_in[:, c2:]
    return w_in, w_rope, w_q, w_kv


def _prep_w_uq(w_uq):
    w = w_uq.astype(BF16).reshape(MLA_Q_RANK, MLA_HEADS, MLA_QK)
    pe = w[..., MLA_NOPE:]
    w = jnp.concatenate([w[..., :MLA_NOPE], pe, _swap_halves(pe)], axis=-1)
    return w.reshape(MLA_Q_RANK, MLA_HEADS * QK_PAD)


def _prep_w_ukv(w_ukv):
    w = w_ukv.astype(BF16).reshape(MLA_KV_RANK, MLA_HEADS, MLA_NOPE + MLA_V)
    kn = w[..., :MLA_NOPE].reshape(MLA_KV_RANK, MLA_HEADS * MLA_NOPE)
    v = w[..., MLA_NOPE:].reshape(MLA_KV_RANK, MLA_HEADS * MLA_V)
    return jnp.concatenate([kn, v], axis=1)


def _swa_perm_rows(a):
    rest = a.shape[1:]
    a = a.reshape((SWA_KV_HEADS, SWA_GROUP, SWA_HEAD_DIM) + rest)
    return jnp.swapaxes(a, 0, 1).reshape((MIX_B,) + rest)


def kernel(x, positions, attn_pre_g, w_in, q_norm_g, w_uq, kv_norm_g, w_ukv, swa_sinks, grp_a_g, grp_b_g,
           w_o, attn_post_g, ffn_pre_g, w_gate, w_up, w_down, ffn_post_g):
    B, S, _ = x.shape
    T = B * S
    depth = w_in.shape[0]
    inv = 1.0 / (ROPE_THETA ** (jnp.arange(0, MLA_ROPE, 2, dtype=F32) / MLA_ROPE))
    zeros64 = jnp.zeros((MLA_ROPE,), F32)
    inv_l = jnp.concatenate([inv, inv, zeros64]).reshape(1, LANES)
    cmask = jnp.concatenate([jnp.ones((MLA_ROPE,), F32), zeros64]).reshape(1, LANES)
    half = MLA_ROPE // 2
    sgn = jnp.concatenate([-jnp.ones((half,), F32), jnp.ones((half,), F32), zeros64]).reshape(1, LANES)
    slopes = jnp.exp2(-8.0 * jnp.arange(1, SWA_Q_HEADS + 1, dtype=F32) / SWA_Q_HEADS)
    pos_col = positions.reshape(T, 1)
    pos_row = positions.reshape(B, 1, S)

    h = x.reshape(T, D_MODEL)
    for l in range(depth):
        w_lat, w_rope, w_q, w_kv = _prep_w_in(w_in[l])
        w_o_bf = w_o[l].astype(BF16)
        w_ob = _swa_perm_rows(w_o_bf[MIX_A:])
        gb = _swa_perm_rows(grp_b_g[l]).reshape(1, MIX_B)
        g_pre = attn_pre_g[l].reshape(1, D_MODEL)

        lat, kr = _in_proj_mla(h, g_pre, w_lat, w_rope)
        proj_swa = _in_proj_swa(h, g_pre, w_q, w_kv)
        q, kv, kpe = _qkv_up(lat, kr, pos_col, inv_l, cmask, sgn, q_norm_g[l].reshape(1, MLA_Q_RANK),
                             kv_norm_g[l].reshape(1, MLA_KV_RANK), _prep_w_uq(w_uq[l]), _prep_w_ukv(w_ukv[l]))
        o_a, wg_bf, wu_bf = _mla_attention(q, kv, kpe, w_gate[l], w_up[l], B, S)
        y_a = _out_proj_a(o_a, grp_a_g[l].reshape(1, MIX_A), w_o_bf)
        o_b, wd_bf = _swa_attention(proj_swa, pos_col, pos_row, slopes, swa_sinks[l].astype(F32),
                                    w_down[l], B, S)
        h = _out_proj_b(o_b, gb, w_ob, y_a, h, attn_post_g[l].reshape(1, D_MODEL))
        h = _ffn(h, ffn_pre_g[l].reshape(1, D_MODEL), wg_bf, wu_bf, wd_bf, ffn_post_g[l].reshape(1, D_MODEL))
    return h.reshape(B, S, D_MODEL)
```

```python
import functools

import jax
import jax.numpy as jnp
from jax import lax
from jax.experimental import pallas as pl
from jax.experimental.pallas import tpu as pltpu

D_MODEL = 4096
MLA_HEADS = 16
MLA_Q_RANK = 1024
MLA_KV_RANK = 512
MLA_NOPE = 128
MLA_ROPE = 64
MLA_V = 128
MLA_QK = MLA_NOPE + MLA_ROPE
ROPE_THETA = 10000.0
SWA_Q_HEADS = 32
SWA_KV_HEADS = 8
SWA_HEAD_DIM = 64
SWA_GROUP = SWA_Q_HEADS // SWA_KV_HEADS
SWA_WINDOW = 128
BLOCK = 128
MIX_A = MLA_HEADS * MLA_V
MIX_B = SWA_Q_HEADS * SWA_HEAD_DIM
D_FF = 11008
EPS = 1e-6

LANES = 128
QK_PAD = 256
KV_W = SWA_KV_HEADS * SWA_HEAD_DIM
LATENT_W = MLA_Q_RANK + MLA_KV_RANK
COL_QS = 0
COL_KS = COL_QS + MIX_B
COL_VS = COL_KS + KV_W
SWA_IN_W = COL_VS + KV_W
NEG = float(jnp.finfo(jnp.float32).min)
VMEM_LIMIT = 56 * 1024 * 1024
DOWN_CHUNK = 1024
FF_TILE = 256
N_CHUNK = 1024
FINISH_ROWS = 64
LOG2_E = 1.4426950408889634
MASKED_DIST = 1e30
F32 = jnp.float32
BF16 = jnp.bfloat16


def _rms(xf, g):
    return xf * lax.rsqrt(jnp.mean(xf * xf, axis=-1, keepdims=True) + EPS) * g


def _row_rsqrt(xf):
    return lax.rsqrt(jnp.mean(xf * xf, axis=-1, keepdims=True) + EPS)


def _for_row_chunks(n_rows, fn, chunk, unroll=1):
    def body(r, carry):
        fn(pl.ds(pl.multiple_of(r * chunk, chunk), chunk))
        return carry
    lax.fori_loop(0, n_rows // chunk, body, 0, unroll=unroll)


def _params(sem):
    return pltpu.CompilerParams(dimension_semantics=sem, vmem_limit_bytes=VMEM_LIMIT)


def _resident(shape, block=None):
    block = (0,) * len(shape) if block is None else block
    return pl.BlockSpec(shape, lambda *_: block, pipeline_mode=pl.Buffered(1))


def _col_chunks(width, chunk=N_CHUNK):
    return [slice(lo, min(lo + chunk, width)) for lo in range(0, width, chunk)]


def _inproj_mla_kernel(x_ref, g_ref, wl_ref, wr_ref, lat_ref, kr_ref):
    x = x_ref[...]
    r = _row_rsqrt(x)
    xg = (x * g_ref[...]).astype(BF16)
    lat_ref[...] = (jnp.dot(xg, wl_ref[...], preferred_element_type=F32) * r).astype(BF16)
    kr_ref[...] = jnp.dot(xg, wr_ref[...], preferred_element_type=F32) * r


def _in_proj_mla(x2, g, w_lat, w_rope, tm=512):
    T = x2.shape[0]
    row = lambda i: (i, 0)
    return pl.pallas_call(
        _inproj_mla_kernel,
        grid=(T // tm,),
        in_specs=[
            pl.BlockSpec((tm, D_MODEL), row),
            _resident((1, D_MODEL)),
            _resident((D_MODEL, LATENT_W)),
            _resident((D_MODEL, LANES)),
        ],
        out_specs=[pl.BlockSpec((tm, LATENT_W), row), pl.BlockSpec((tm, LANES), row)],
        out_shape=[jax.ShapeDtypeStruct((T, LATENT_W), BF16), jax.ShapeDtypeStruct((T, LANES), F32)],
        compiler_params=_params(("parallel",)),
        name="in_proj_mla",
    )(x2, g, w_lat, w_rope)


def _inproj_swa_kernel(x_ref, g_ref, wq_ref, wkv_ref, wo_ref, o_ref, wo_bf_ref):
    wo_bf_ref[...] = wo_ref[...].astype(BF16)
    x = x_ref[...]
    r = _row_rsqrt(x)
    xg = (x * g_ref[...]).astype(BF16)
    rq = r * (SWA_HEAD_DIM ** -0.5 * LOG2_E)
    for cs in _col_chunks(MIX_B):
        o_ref[:, cs] = (jnp.dot(xg, wq_ref[:, cs], preferred_element_type=F32) * rq).astype(BF16)
    o_ref[:, COL_KS:] = (jnp.dot(xg, wkv_ref[...], preferred_element_type=F32) * r).astype(BF16)


def _in_proj_swa(x2, g, w_q, w_kv, w_o):
    T = x2.shape[0]
    n_slabs = (MIX_A + MIX_B) // SWA_HEAD_DIM
    n_top = MIX_A // SWA_HEAD_DIM
    tm = T // n_slabs
    row = lambda i: (i, 0)

    def slab_dst(i):
        r = jnp.maximum(i - n_top, 0)
        return jnp.where(i < n_top, i, n_top + (r % SWA_GROUP) * SWA_KV_HEADS + r // SWA_GROUP), 0

    return pl.pallas_call(
        _inproj_swa_kernel,
        grid=(n_slabs,),
        in_specs=[
            pl.BlockSpec((tm, D_MODEL), row),
            _resident((1, D_MODEL)),
            _resident((D_MODEL, MIX_B)),
            _resident((D_MODEL, 2 * KV_W)),
            pl.BlockSpec((SWA_HEAD_DIM, D_MODEL), row),
        ],
        out_specs=[pl.BlockSpec((tm, SWA_IN_W), row), pl.BlockSpec((SWA_HEAD_DIM, D_MODEL), slab_dst)],
        out_shape=[jax.ShapeDtypeStruct((T, SWA_IN_W), BF16),
                   jax.ShapeDtypeStruct((MIX_A + MIX_B, D_MODEL), BF16)],
        compiler_params=_params(("arbitrary",)),
        name="in_proj_swa",
    )(x2, g, w_q, w_kv, w_o)


def _qkvup_kernel(cq_ref, ckv_ref, kr_ref, pos_ref, inv_ref, cmask_ref, sgn_ref, gq_ref, gkv_ref,
                  wq_ref, wkv_ref, q_ref, kv_ref, kpe_ref):
    ang = pos_ref[...].astype(F32) * inv_ref[...]
    c = jnp.cos(ang) * cmask_ref[...]
    s = jnp.sin(ang) * sgn_ref[...]
    kr = kr_ref[...]
    kpe_ref[...] = (kr * c + pltpu.roll(kr, 64, 1) * s).astype(BF16)
    scale = MLA_QK ** -0.5 * LOG2_E
    cq = cq_ref[...].astype(F32)
    rq = _row_rsqrt(cq) * scale
    cqg = (cq * gq_ref[...]).astype(BF16)
    for cs in _col_chunks(MLA_HEADS * QK_PAD):
        y = jnp.dot(cqg, wq_ref[:, cs], preferred_element_type=F32) * rq
        for h in range((cs.stop - cs.start) // QK_PAD):
            lo = h * QK_PAD
            q_ref[:, cs.start + lo:cs.start + lo + MLA_NOPE] = y[:, lo:lo + MLA_NOPE].astype(BF16)
            pe = y[:, lo + MLA_NOPE:lo + QK_PAD]
            q_ref[:, cs.start + lo + MLA_NOPE:cs.start + lo + QK_PAD] = (
                pe * c + pltpu.roll(pe, 64, 1) * s).astype(BF16)
    ckv = ckv_ref[...].astype(F32)
    rkv = _row_rsqrt(ckv)
    ckvg = (ckv * gkv_ref[...]).astype(BF16)
    for cs in _col_chunks(MLA_HEADS * (MLA_NOPE + MLA_V)):
        kv_ref[:, cs] = (jnp.dot(ckvg, wkv_ref[:, cs], preferred_element_type=F32) * rkv).astype(BF16)


def _qkv_up(lat, kr, pos_col, inv, cmask, sgn, gq, gkv, w_uq_r, w_ukv_r, tm=512):
    T = lat.shape[0]
    nq = MLA_HEADS * QK_PAD
    nkv = MLA_HEADS * (MLA_NOPE + MLA_V)
    row = lambda i: (i, 0)
    return pl.pallas_call(
        _qkvup_kernel,
        grid=(T // tm,),
        in_specs=[
            pl.BlockSpec((tm, MLA_Q_RANK), row),
            pl.BlockSpec((tm, MLA_KV_RANK), lambda i: (i, MLA_Q_RANK // MLA_KV_RANK)),
            pl.BlockSpec((tm, LANES), row),
            pl.BlockSpec((tm, 1), row),
            _resident((1, LANES)),
            _resident((1, LANES)),
            _resident((1, LANES)),
            _resident((1, MLA_Q_RANK)),
            _resident((1, MLA_KV_RANK)),
            _resident((MLA_Q_RANK, nq)),
            _resident((MLA_KV_RANK, nkv)),
        ],
        out_specs=[pl.BlockSpec((tm, nq), row), pl.BlockSpec((tm, nkv), row), pl.BlockSpec((tm, LANES), row)],
        out_shape=[jax.ShapeDtypeStruct((T, nq), BF16), jax.ShapeDtypeStruct((T, nkv), BF16),
                   jax.ShapeDtypeStruct((T, LANES), BF16)],
        compiler_params=_params(("parallel",)),
        name="qkv_up",
    )(lat, lat, kr, pos_col, inv, cmask, sgn, gq, gkv, w_uq_r, w_ukv_r)


def _mla_kernel(q_ref, kn_ref, kpe_ref, v_ref, wg_ref, wu_ref, o_ref, wg_bf_ref, wu_bf_ref,
                m_ref, l_ref, acc_ref, *, th, heads):
    wg_bf_ref[...] = wg_ref[...].astype(BF16)
    wu_bf_ref[...] = wu_ref[...].astype(BF16)
    qi = pl.program_id(2)
    m_ref[...] = jnp.full(m_ref.shape, NEG, F32)
    l_ref[...] = jnp.zeros(l_ref.shape, F32)
    acc_ref[...] = jnp.zeros(acc_ref.shape, F32)

    def load_kv(head, start, size):
        rows = pl.ds(pl.multiple_of(start, size), size)
        k = jnp.concatenate([kn_ref[rows, head * MLA_NOPE:(head + 1) * MLA_NOPE], kpe_ref[rows, :]], axis=1)
        return k, v_ref[rows, head * MLA_V:(head + 1) * MLA_V]

    def update(head, half, k, v, diag_col=None):
        chain = 2 * head + half
        q = q_ref[half * th:(half + 1) * th, head * QK_PAD:(head + 1) * QK_PAD]
        s = lax.dot_general(q, k, (((1,), (1,)), ((), ())), preferred_element_type=F32)
        chunks = [s[:, c * LANES:(c + 1) * LANES] for c in range(s.shape[1] // LANES)]
        if diag_col is not None:
            row = lax.broadcasted_iota(jnp.int32, (th, LANES), 0)
            col = lax.broadcasted_iota(jnp.int32, (th, LANES), 1)
            for c in range(diag_col // LANES, len(chunks)):
                chunks[c] = jnp.where(col + (c * LANES - diag_col) <= row, chunks[c], NEG)
        mx = chunks[0]
        for c in chunks[1:]:
            mx = jnp.maximum(mx, c)
        m_prev = m_ref[chain]
        m_new = jnp.maximum(m_prev, jnp.max(mx, axis=-1, keepdims=True))
        alpha = jnp.exp2(m_prev - m_new)
        ps = [jnp.exp2(c - m_new) for c in chunks]
        psum = ps[0]
        for p in ps[1:]:
            psum = psum + p
        l_ref[chain] = alpha * l_ref[chain] + psum
        p = jnp.concatenate(ps, axis=1).astype(BF16)
        acc_ref[chain] = alpha * acc_ref[chain] + jnp.dot(p, v, preferred_element_type=F32)
        m_ref[chain] = m_new

    def body(j, carry):
        for head in range(heads):
            k, v = load_kv(head, j * tq, tq)
            update(head, 0, k, v)
            update(head, 1, k, v)
        return carry

    tq = 2 * th
    lax.fori_loop(0, qi, body, 0)
    for head in range(heads):
        k, v = load_kv(head, qi * tq, tq)
        update(head, 0, k[:th], v[:th], diag_col=0)
        update(head, 1, k, v, diag_col=th)
    for head in range(heads):
        for half in range(2):
            l = jnp.sum(l_ref[2 * head + half], axis=-1, keepdims=True)
            o_ref[half * th:(half + 1) * th, head * MLA_V:(head + 1) * MLA_V] = (
                acc_ref[2 * head + half] / l).astype(BF16)


def _mla_attention(q, kv, kpe, w_gate, w_up, B, S, th=512, heads=4):
    T = B * S
    tq = 2 * th
    nq = S // tq
    n_groups = MLA_HEADS // heads
    slab = D_MODEL // (B * n_groups * nq)
    w_spec = pl.BlockSpec((slab, D_FF), lambda b, h, i: ((b * n_groups + h) * nq + i, 0))
    return pl.pallas_call(
        functools.partial(_mla_kernel, th=th, heads=heads),
        grid=(B, n_groups, nq),
        in_specs=[
            pl.BlockSpec((tq, heads * QK_PAD), lambda b, h, i: (b * nq + i, h)),
            pl.BlockSpec((S, heads * MLA_NOPE), lambda b, h, i: (b, h)),
            pl.BlockSpec((S, LANES), lambda b, h, i: (b, 0)),
            pl.BlockSpec((S, heads * MLA_V), lambda b, h, i: (b, n_groups + h)),
            w_spec, w_spec,
        ],
        out_specs=[pl.BlockSpec((tq, heads * MLA_V), lambda b, h, i: (b * nq + i, h)), w_spec, w_spec],
        out_shape=[jax.ShapeDtypeStruct((T, MIX_A), BF16),
                   jax.ShapeDtypeStruct((D_MODEL, D_FF), BF16),
                   jax.ShapeDtypeStruct((D_MODEL, D_FF), BF16)],
        scratch_shapes=[
            pltpu.VMEM((2 * heads, th, LANES), F32),
            pltpu.VMEM((2 * heads, th, LANES), F32),
            pltpu.VMEM((2 * heads, th, MLA_V), F32),
        ],
        compiler_params=_params(("arbitrary", "arbitrary", "arbitrary")),
        name="mla_attn",
    )(q, kv, kpe, kv, w_gate, w_up)


def _swa_kernel(slope_ref, sink_ref, q0_ref, q1_ref, q2_ref, q3_ref, k_ref, kp_ref, v_ref, vp_ref,
                pc_ref, pr_ref, prp_ref, wd_ref, o_ref, wd_bf_ref, *, tq):
    wd_bf_ref[...] = wd_ref[...].astype(BF16)
    n = pl.program_id(1)
    pair = pl.program_id(2)
    q_refs = (q0_ref, q1_ref, q2_ref, q3_ref)
    lane = lax.broadcasted_iota(jnp.int32, (2 * BLOCK, LANES), 1)
    lo = lane < SWA_HEAD_DIM
    k_all = jnp.concatenate([kp_ref[...], k_ref[...]], axis=0)
    v_all = jnp.concatenate([vp_ref[...], v_ref[...]], axis=0)
    pos_k = jnp.concatenate([prp_ref[0], pr_ref[0]], axis=1)
    i_idx = lax.broadcasted_iota(jnp.int32, (BLOCK, 2 * BLOCK), 0)
    c_idx = lax.broadcasted_iota(jnp.int32, (BLOCK, 2 * BLOCK), 1)
    in_window = (c_idx > i_idx) & (c_idx <= i_idx + SWA_WINDOW)
    zero = jnp.zeros((), BF16)
    lo_out = lax.broadcasted_iota(jnp.int32, (BLOCK, LANES), 1) < SWA_HEAD_DIM

    for t in range(tq // BLOCK):
        r0 = t * BLOCK
        band_k = k_all[r0:r0 + 2 * BLOCK]
        band_v = v_all[r0:r0 + 2 * BLOCK]
        k_bd = jnp.concatenate([jnp.where(lo, band_k, zero), jnp.where(lo, zero, band_k)], axis=0)
        v_bd = jnp.concatenate([jnp.where(lo, band_v, zero), jnp.where(lo, zero, band_v)], axis=0)
        q_st = jnp.concatenate([qr[r0:r0 + BLOCK, :] for qr in q_refs], axis=0)
        s = lax.dot_general(q_st, k_bd, (((1,), (1,)), ((), ())), preferred_element_type=F32)
        dist = jnp.abs(pc_ref[r0:r0 + BLOCK, :] - pos_k[:, r0:r0 + 2 * BLOCK]).astype(F32)
        valid = in_window
        if t == 0:
            valid = valid & ((c_idx >= BLOCK) | (n > 0))
        dist = jnp.where(valid, dist, MASKED_DIST)
        rows = []
        inv_den = []
        for g in range(SWA_GROUP):
            halves = []
            for c in range(2):
                head = (2 * pair + c) * SWA_GROUP + g
                slope = slope_ref[head] * LOG2_E
                sink = sink_ref[head] * LOG2_E
                sg = s[g * BLOCK:(g + 1) * BLOCK, c * 2 * BLOCK:(c + 1) * 2 * BLOCK] - slope * dist
                m = jnp.maximum(jnp.max(sg, axis=-1, keepdims=True), sink)
                e = jnp.exp2(sg - m)
                den = jnp.sum(e, axis=-1, keepdims=True) + jnp.exp2(sink - m)
                halves.append(e.astype(BF16))
                inv_den.append(1.0 / den)
            rows.append(jnp.concatenate(halves, axis=1))
        p = jnp.concatenate(rows, axis=0)
        out = jnp.dot(p, v_bd, preferred_element_type=F32)
        for g in range(SWA_GROUP):
            norm = jnp.where(lo_out, inv_den[2 * g], inv_den[2 * g + 1])
            o_ref[g, r0:r0 + BLOCK, :] = (out[g * BLOCK:(g + 1) * BLOCK] * norm).astype(BF16)


def _swa_attention(proj, pos_col, pos_row, slopes, sinks, w_down, B, S, tq=1024):
    T = B * S
    nt = S // tq
    sub = tq // BLOCK
    npairs = SWA_KV_HEADS // 2
    n_steps = B * nt * npairs
    slab = FF_TILE if n_steps >= D_FF // FF_TILE else D_FF
    last_slab = D_FF // slab - 1
    wd_spec = pl.BlockSpec(
        (slab, D_MODEL), lambda b, n, j: (jnp.minimum((b * nt + n) * npairs + j, last_slab), 0))
    qs0 = COL_QS // LANES
    ks0 = COL_KS // LANES
    vs0 = COL_VS // LANES
    pairs_per_group = KV_W // LANES

    def q_spec(g):
        return pl.BlockSpec((tq, LANES), lambda b, n, j, g=g: (b * nt + n, qs0 + pairs_per_group * g + j))

    def prev_row(b, n):
        return jnp.maximum(b * (S // BLOCK) + sub * n - 1, 0)

    smem = pl.BlockSpec(memory_space=pltpu.SMEM)
    return pl.pallas_call(
        functools.partial(_swa_kernel, tq=tq),
        grid=(B, nt, npairs),
        in_specs=[
            smem, smem,
            q_spec(0), q_spec(1), q_spec(2), q_spec(3),
            pl.BlockSpec((tq, LANES), lambda b, n, j: (b * nt + n, ks0 + j)),
            pl.BlockSpec((BLOCK, LANES), lambda b, n, j: (prev_row(b, n), ks0 + j)),
            pl.BlockSpec((tq, LANES), lambda b, n, j: (b * nt + n, vs0 + j)),
            pl.BlockSpec((BLOCK, LANES), lambda b, n, j: (prev_row(b, n), vs0 + j)),
            pl.BlockSpec((tq, 1), lambda b, n, j: (b * nt + n, 0)),
            pl.BlockSpec((1, 1, tq), lambda b, n, j: (b, 0, n)),
            pl.BlockSpec((1, 1, BLOCK), lambda b, n, j: (b, 0, jnp.maximum(sub * n - 1, 0))),
            wd_spec,
        ],
        out_specs=[pl.BlockSpec((SWA_GROUP, tq, LANES), lambda b, n, j: (0, b * nt + n, j)), wd_spec],
        out_shape=[jax.ShapeDtypeStruct((SWA_GROUP, T, KV_W), BF16),
                   jax.ShapeDtypeStruct((D_FF, D_MODEL), BF16)],
        compiler_params=_params(("arbitrary", "arbitrary", "arbitrary")),
        name="swa_attn",
    )(slopes, sinks, proj, proj, proj, proj, proj, proj, proj, proj, pos_col, pos_row, pos_row, w_down)


def _oproj_a_kernel(oa_ref, ga_ref, w_ref, y_ref):
    oa = oa_ref[...].astype(F32)
    r = _row_rsqrt(oa)
    og = (oa * ga_ref[...]).astype(BF16)
    for cs in _col_chunks(D_MODEL):
        y_ref[:, cs] = jnp.dot(og, w_ref[:, cs], preferred_element_type=F32) * r


def _out_proj_a(o_a, ga, w_oa, tm=512):
    T = o_a.shape[0]
    row = lambda i: (i, 0)
    return pl.pallas_call(
        _oproj_a_kernel,
        grid=(T // tm,),
        in_specs=[pl.BlockSpec((tm, MIX_A), row), _resident((1, MIX_A)), _resident((MIX_A, D_MODEL))],
        out_specs=pl.BlockSpec((tm, D_MODEL), row),
        out_shape=jax.ShapeDtypeStruct((T, D_MODEL), F32),
        compiler_params=_params(("parallel",)),
        name="out_proj_a",
    )(o_a, ga, w_oa)


def _oproj_b_kernel(ob_ref, gb_ref, w_ref, ya_ref, x_ref, gp_ref, h_ref):
    ob = jnp.concatenate([ob_ref[g] for g in range(SWA_GROUP)], axis=1).astype(F32)
    r = _row_rsqrt(ob)
    og = (ob * gb_ref[...]).astype(BF16)
    ssq = jnp.zeros((h_ref.shape[0], 1), F32)
    for cs in _col_chunks(D_MODEL):
        y = ya_ref[:, cs] + jnp.dot(og, w_ref[:, cs], preferred_element_type=F32) * r
        ssq = ssq + jnp.sum(y * y, axis=-1, keepdims=True)
        h_ref[:, cs] = y
    rn = lax.rsqrt(ssq * (1.0 / D_MODEL) + EPS)
    for cs in _col_chunks(D_MODEL):
        h_ref[:, cs] = x_ref[:, cs] + h_ref[:, cs] * rn * gp_ref[:, cs]


def _out_proj_b(o_b, gb, w_o_bf, y_a, x2, gpost, tm=256):
    T = x2.shape[0]
    row = lambda i: (i, 0)
    return pl.pallas_call(
        _oproj_b_kernel,
        grid=(T // tm,),
        in_specs=[
            pl.BlockSpec((SWA_GROUP, tm, KV_W), lambda i: (0, i, 0)),
            _resident((1, MIX_B)),
            _resident((MIX_B, D_MODEL), block=(MIX_A // MIX_B, 0)),
            pl.BlockSpec((tm, D_MODEL), row),
            pl.BlockSpec((tm, D_MODEL), row),
            _resident((1, D_MODEL)),
        ],
        out_specs=pl.BlockSpec((tm, D_MODEL), row),
        out_shape=jax.ShapeDtypeStruct((T, D_MODEL), F32),
        compiler_params=_params(("parallel",)),
        name="out_proj_b",
    )(o_b, gb, w_o_bf, y_a, x2, gpost)


def _ffn_kernel(h_ref, gpre_ref, wg_ref, wu_ref, wd_ref, gpost_ref, o_ref, f_ref, *, nj):
    j = pl.program_id(1)

    @pl.when(j == 0)
    def _():
        def norm_rows(rows):
            f_ref[rows, :] = _rms(h_ref[rows, :], gpre_ref[...]).astype(BF16)
            o_ref[rows, :] = jnp.zeros((FINISH_ROWS, D_MODEL), F32)
        _for_row_chunks(f_ref.shape[0], norm_rows, FINISH_ROWS)

    f = f_ref[...]
    gate = jnp.dot(f, wg_ref[...], preferred_element_type=F32)
    up = jnp.dot(f, wu_ref[...], preferred_element_type=F32)
    act = (gate * jax.nn.sigmoid(gate) * up).astype(BF16)
    for cs in _col_chunks(D_MODEL, DOWN_CHUNK):
        o_ref[:, cs] += jnp.dot(act, wd_ref[:, cs], preferred_element_type=F32)

    @pl.when(j == nj - 1)
    def _():
        def finish_rows(rows):
            o_ref[rows, :] = h_ref[rows, :] + _rms(o_ref[rows, :], gpost_ref[...])
        _for_row_chunks(o_ref.shape[0], finish_rows, FINISH_ROWS)


def _ffn(h, gpre, wg, wu, wd, gpost, tm=512, tf=FF_TILE):
    T = h.shape[0]
    nj = D_FF // tf
    return pl.pallas_call(
        functools.partial(_ffn_kernel, nj=nj),
        grid=(T // tm, nj),
        in_specs=[
            pl.BlockSpec((tm, D_MODEL), lambda i, j: (i, 0)),
            pl.BlockSpec((1, D_MODEL), lambda i, j: (0, 0)),
            pl.BlockSpec((D_MODEL, tf), lambda i, j: (0, j)),
            pl.BlockSpec((D_MODEL, tf), lambda i, j: (0, j)),
            pl.BlockSpec((tf, D_MODEL), lambda i, j: (j, 0)),
            pl.BlockSpec((1, D_MODEL), lambda i, j: (0, 0)),
        ],
        out_specs=pl.BlockSpec((tm, D_MODEL), lambda i, j: (i, 0)),
        out_shape=jax.ShapeDtypeStruct((T, D_MODEL), F32),
        scratch_shapes=[pltpu.VMEM((tm, D_MODEL), BF16)],
        compiler_params=_params(("parallel", "arbitrary")),
        name="ffn",
    )(h, gpre, wg, wu, wd, gpost)


def _swap_halves(w):
    half = w.shape[-1] // 2
    return jnp.concatenate([w[..., half:], w[..., :half]], axis=-1)


def _w_in_prep_kernel(w_ref, lat_ref, rope_ref, q_ref, kv_ref):
    c0 = LATENT_W
    c1 = c0 + MLA_ROPE
    c2 = c1 + MIX_B
    half = MLA_ROPE // 2
    w = w_ref[...]
    lat_ref[...] = w[:, :c0].astype(BF16)
    rope_ref[...] = jnp.concatenate([w[:, c0:c1], w[:, c0 + half:c1], w[:, c0:c0 + half]], axis=1).astype(BF16)
    for g in range(SWA_GROUP):
        for pair in range(SWA_KV_HEADS // 2):
            srcs = [c1 + ((2 * pair + c) * SWA_GROUP + g) * SWA_HEAD_DIM for c in range(2)]
            dst = (g * SWA_KV_HEADS + 2 * pair) * SWA_HEAD_DIM
            q_ref[:, dst:dst + LANES] = jnp.concatenate(
                [w[:, a:a + SWA_HEAD_DIM] for a in srcs], axis=1).astype(BF16)
    kv_ref[...] = w[:, c2:].astype(BF16)


def _prep_w_in(w_in, rows=128):
    row = lambda i: (i, 0)
    widths = (LATENT_W, LANES, MIX_B, 2 * KV_W)
    return pl.pallas_call(
        _w_in_prep_kernel,
        grid=(D_MODEL // rows,),
        in_specs=[pl.BlockSpec((rows, w_in.shape[1]), row)],
        out_specs=[pl.BlockSpec((rows, n), row) for n in widths],
        out_shape=[jax.ShapeDtypeStruct((D_MODEL, n), BF16) for n in widths],
        compiler_params=_params(("parallel",)),
        name="w_in_prep",
    )(w_in)


def _prep_w_uq(w_uq):
    w = w_uq.astype(BF16).reshape(MLA_Q_RANK, MLA_HEADS, MLA_QK)
    pe = w[..., MLA_NOPE:]
    w = jnp.concatenate([w[..., :MLA_NOPE], pe, _swap_halves(pe)], axis=-1)
    return w.reshape(MLA_Q_RANK, MLA_HEADS * QK_PAD)


def _prep_w_ukv(w_ukv):
    w = w_ukv.astype(BF16).reshape(MLA_KV_RANK, MLA_HEADS, MLA_NOPE + MLA_V)
    kn = w[..., :MLA_NOPE].reshape(MLA_KV_RANK, MLA_HEADS * MLA_NOPE)
    v = w[..., MLA_NOPE:].reshape(MLA_KV_RANK, MLA_HEADS * MLA_V)
    return jnp.concatenate([kn, v], axis=1)


def _swa_perm_rows(a):
    rest = a.shape[1:]
    a = a.reshape((SWA_KV_HEADS, SWA_GROUP, SWA_HEAD_DIM) + rest)
    return jnp.swapaxes(a, 0, 1).reshape((MIX_B,) + rest)


def kernel(x, positions, attn_pre_g, w_in, q_norm_g, w_uq, kv_norm_g, w_ukv, swa_sinks, grp_a_g, grp_b_g,
           w_o, attn_post_g, ffn_pre_g, w_gate, w_up, w_down, ffn_post_g):
    B, S, _ = x.shape
    T = B * S
    depth = w_in.shape[0]
    inv = 1.0 / (ROPE_THETA ** (jnp.arange(0, MLA_ROPE, 2, dtype=F32) / MLA_ROPE))
    zeros64 = jnp.zeros((MLA_ROPE,), F32)
    inv_l = jnp.concatenate([inv, inv, zeros64]).reshape(1, LANES)
    cmask = jnp.concatenate([jnp.ones((MLA_ROPE,), F32), zeros64]).reshape(1, LANES)
    half = MLA_ROPE // 2
    sgn = jnp.concatenate([-jnp.ones((half,), F32), jnp.ones((half,), F32), zeros64]).reshape(1, LANES)
    slopes = jnp.exp2(-8.0 * jnp.arange(1, SWA_Q_HEADS + 1, dtype=F32) / SWA_Q_HEADS)
    pos_col = positions.reshape(T, 1)
    pos_row = positions.reshape(B, 1, S)

    h = x.reshape(T, D_MODEL)
    for l in range(depth):
        w_lat, w_rope, w_q, w_kv = _prep_w_in(w_in[l])
        gb = _swa_perm_rows(grp_b_g[l]).reshape(1, MIX_B)
        g_pre = attn_pre_g[l].reshape(1, D_MODEL)

        lat, kr = _in_proj_mla(h, g_pre, w_lat, w_rope)
        proj_swa, w_o_bf = _in_proj_swa(h, g_pre, w_q, w_kv, w_o[l])
        q, kv, kpe = _qkv_up(lat, kr, pos_col, inv_l, cmask, sgn, q_norm_g[l].reshape(1, MLA_Q_RANK),
                             kv_norm_g[l].reshape(1, MLA_KV_RANK), _prep_w_uq(w_uq[l]), _prep_w_ukv(w_ukv[l]))
        o_a, wg_bf, wu_bf = _mla_attention(q, kv, kpe, w_gate[l], w_up[l], B, S)
        y_a = _out_proj_a(o_a, grp_a_g[l].reshape(1, MIX_A), w_o_bf)
        o_b, wd_bf = _swa_attention(proj_swa, pos_col, pos_row, slopes, swa_sinks[l].astype(F32),
                                    w_down[l], B, S)
        h = _out_proj_b(o_b, gb, w_o_bf, y_a, h, attn_post_g[l].reshape(1, D_MODEL))
        h = _ffn(h, ffn_pre_g[l].reshape(1, D_MODEL), wg_bf, wu_bf, wd_bf, ffn_post_g[l].reshape(1, D_MODEL))
    return h.reshape(B, S, D_MODEL)
```

```python
import functools

import jax
import jax.numpy as jnp
from jax import lax
from jax.experimental import pallas as pl
from jax.experimental.pallas import tpu as pltpu

D_MODEL = 4096
MLA_HEADS = 16
MLA_Q_RANK = 1024
MLA_KV_RANK = 512
MLA_NOPE = 128
MLA_ROPE = 64
MLA_V = 128
MLA_QK = MLA_NOPE + MLA_ROPE
ROPE_THETA = 10000.0
SWA_Q_HEADS = 32
SWA_KV_HEADS = 8
SWA_HEAD_DIM = 64
SWA_GROUP = SWA_Q_HEADS // SWA_KV_HEADS
SWA_WINDOW = 128
BLOCK = 128
MIX_A = MLA_HEADS * MLA_V
MIX_B = SWA_Q_HEADS * SWA_HEAD_DIM
D_FF = 11008
EPS = 1e-6

LANES = 128
QK_PAD = 256
KV_W = SWA_KV_HEADS * SWA_HEAD_DIM
LATENT_W = MLA_Q_RANK + MLA_KV_RANK
COL_QS = 0
COL_KS = COL_QS + MIX_B
COL_VS = COL_KS + KV_W
SWA_IN_W = COL_VS + KV_W
NEG = float(jnp.finfo(jnp.float32).min)
VMEM_LIMIT = 56 * 1024 * 1024
DOWN_CHUNK = 1024
FF_TILE = 256
N_CHUNK = 1024
FINISH_ROWS = 64
LOG2_E = 1.4426950408889634
MASKED_DIST = 1e30
F32 = jnp.float32
BF16 = jnp.bfloat16


def _rms(xf, g):
    return xf * lax.rsqrt(jnp.mean(xf * xf, axis=-1, keepdims=True) + EPS) * g


def _row_rsqrt(xf):
    return lax.rsqrt(jnp.mean(xf * xf, axis=-1, keepdims=True) + EPS)


def _for_row_chunks(n_rows, fn, chunk, unroll=1):
    def body(r, carry):
        fn(pl.ds(pl.multiple_of(r * chunk, chunk), chunk))
        return carry
    lax.fori_loop(0, n_rows // chunk, body, 0, unroll=unroll)


def _params(sem):
    return pltpu.CompilerParams(dimension_semantics=sem, vmem_limit_bytes=VMEM_LIMIT)


def _resident(shape, block=None):
    block = (0,) * len(shape) if block is None else block
    return pl.BlockSpec(shape, lambda *_: block, pipeline_mode=pl.Buffered(1))


def _col_chunks(width, chunk=N_CHUNK):
    return [slice(lo, min(lo + chunk, width)) for lo in range(0, width, chunk)]


def _inproj_mla_kernel(x_ref, g_ref, wl_ref, wr_ref, lat_ref, kr_ref):
    x = x_ref[...]
    r = _row_rsqrt(x)
    xg = (x * g_ref[...]).astype(BF16)
    lat_ref[...] = (jnp.dot(xg, wl_ref[...], preferred_element_type=F32) * r).astype(BF16)
    kr_ref[...] = jnp.dot(xg, wr_ref[...], preferred_element_type=F32) * r


def _in_proj_mla(x2, g, w_lat, w_rope, tm=512):
    T = x2.shape[0]
    row = lambda i: (i, 0)
    return pl.pallas_call(
        _inproj_mla_kernel,
        grid=(T // tm,),
        in_specs=[
            pl.BlockSpec((tm, D_MODEL), row),
            _resident((1, D_MODEL)),
            _resident((D_MODEL, LATENT_W)),
            _resident((D_MODEL, LANES)),
        ],
        out_specs=[pl.BlockSpec((tm, LATENT_W), row), pl.BlockSpec((tm, LANES), row)],
        out_shape=[jax.ShapeDtypeStruct((T, LATENT_W), BF16), jax.ShapeDtypeStruct((T, LANES), F32)],
        compiler_params=_params(("parallel",)),
        name="in_proj_mla",
    )(x2, g, w_lat, w_rope)


def _inproj_swa_kernel(x_ref, g_ref, wq_ref, wkv_ref, wo_ref, o_ref, wo_bf_ref):
    wo_bf_ref[...] = wo_ref[...].astype(BF16)
    x = x_ref[...]
    r = _row_rsqrt(x)
    xg = (x * g_ref[...]).astype(BF16)
    rq = r * (SWA_HEAD_DIM ** -0.5 * LOG2_E)
    for cs in _col_chunks(MIX_B):
        o_ref[:, cs] = (jnp.dot(xg, wq_ref[:, cs], preferred_element_type=F32) * rq).astype(BF16)
    o_ref[:, COL_KS:] = (jnp.dot(xg, wkv_ref[...], preferred_element_type=F32) * r).astype(BF16)


def _in_proj_swa(x2, g, w_q, w_kv, w_o):
    T = x2.shape[0]
    n_slabs = (MIX_A + MIX_B) // SWA_HEAD_DIM
    n_top = MIX_A // SWA_HEAD_DIM
    tm = T // n_slabs
    row = lambda i: (i, 0)

    def slab_dst(i):
        r = jnp.maximum(i - n_top, 0)
        return jnp.where(i < n_top, i, n_top + (r % SWA_GROUP) * SWA_KV_HEADS + r // SWA_GROUP), 0

    return pl.pallas_call(
        _inproj_swa_kernel,
        grid=(n_slabs,),
        in_specs=[
            pl.BlockSpec((tm, D_MODEL), row),
            _resident((1, D_MODEL)),
            _resident((D_MODEL, MIX_B)),
            _resident((D_MODEL, 2 * KV_W)),
            pl.BlockSpec((SWA_HEAD_DIM, D_MODEL), row),
        ],
        out_specs=[pl.BlockSpec((tm, SWA_IN_W), row), pl.BlockSpec((SWA_HEAD_DIM, D_MODEL), slab_dst)],
        out_shape=[jax.ShapeDtypeStruct((T, SWA_IN_W), BF16),
                   jax.ShapeDtypeStruct((MIX_A + MIX_B, D_MODEL), BF16)],
        compiler_params=_params(("arbitrary",)),
        name="in_proj_swa",
    )(x2, g, w_q, w_kv, w_o)


def _qkvup_kernel(cq_ref, ckv_ref, kr_ref, pos_ref, inv_ref, cmask_ref, sgn_ref, gq_ref, gkv_ref,
                  wq_ref, wkv_ref, q_ref, kv_ref, kpe_ref):
    ang = pos_ref[...].astype(F32) * inv_ref[...]
    c = jnp.cos(ang) * cmask_ref[...]
    s = jnp.sin(ang) * sgn_ref[...]
    kr = kr_ref[...]
    kpe_ref[...] = (kr * c + pltpu.roll(kr, 64, 1) * s).astype(BF16)
    scale = MLA_QK ** -0.5 * LOG2_E
    cq = cq_ref[...].astype(F32)
    rq = _row_rsqrt(cq) * scale
    cqg = (cq * gq_ref[...]).astype(BF16)
    for cs in _col_chunks(MLA_HEADS * QK_PAD):
        y = jnp.dot(cqg, wq_ref[:, cs], preferred_element_type=F32) * rq
        for h in range((cs.stop - cs.start) // QK_PAD):
            lo = h * QK_PAD
            q_ref[:, cs.start + lo:cs.start + lo + MLA_NOPE] = y[:, lo:lo + MLA_NOPE].astype(BF16)
            pe = y[:, lo + MLA_NOPE:lo + QK_PAD]
            q_ref[:, cs.start + lo + MLA_NOPE:cs.start + lo + QK_PAD] = (
                pe * c + pltpu.roll(pe, 64, 1) * s).astype(BF16)
    ckv = ckv_ref[...].astype(F32)
    rkv = _row_rsqrt(ckv)
    ckvg = (ckv * gkv_ref[...]).astype(BF16)
    for cs in _col_chunks(MLA_HEADS * (MLA_NOPE + MLA_V)):
        kv_ref[:, cs] = (jnp.dot(ckvg, wkv_ref[:, cs], preferred_element_type=F32) * rkv).astype(BF16)


def _qkv_up(lat, kr, pos_col, inv, cmask, sgn, gq, gkv, w_uq_r, w_ukv_r, tm=512):
    T = lat.shape[0]
    nq = MLA_HEADS * QK_PAD
    nkv = MLA_HEADS * (MLA_NOPE + MLA_V)
    row = lambda i: (i, 0)
    return pl.pallas_call(
        _qkvup_kernel,
        grid=(T // tm,),
        in_specs=[
            pl.BlockSpec((tm, MLA_Q_RANK), row),
            pl.BlockSpec((tm, MLA_KV_RANK), lambda i: (i, MLA_Q_RANK // MLA_KV_RANK)),
            pl.BlockSpec((tm, LANES), row),
            pl.BlockSpec((tm, 1), row),
            _resident((1, LANES)),
            _resident((1, LANES)),
            _resident((1, LANES)),
            _resident((1, MLA_Q_RANK)),
            _resident((1, MLA_KV_RANK)),
            _resident((MLA_Q_RANK, nq)),
            _resident((MLA_KV_RANK, nkv)),
        ],
        out_specs=[pl.BlockSpec((tm, nq), row), pl.BlockSpec((tm, nkv), row), pl.BlockSpec((tm, LANES), row)],
        out_shape=[jax.ShapeDtypeStruct((T, nq), BF16), jax.ShapeDtypeStruct((T, nkv), BF16),
                   jax.ShapeDtypeStruct((T, LANES), BF16)],
        compiler_params=_params(("parallel",)),
        name="qkv_up",
    )(lat, lat, kr, pos_col, inv, cmask, sgn, gq, gkv, w_uq_r, w_ukv_r)


def _mla_kernel(q_ref, kn_ref, kpe_ref, v_ref, wg_ref, wu_ref, o_ref, wg_bf_ref, wu_bf_ref,
                m_ref, l_ref, acc_ref, *, th, heads):
    wg_bf_ref[...] = wg_ref[...].astype(BF16)
    wu_bf_ref[...] = wu_ref[...].astype(BF16)
    qi = pl.program_id(2)
    m_ref[...] = jnp.full(m_ref.shape, NEG, F32)
    l_ref[...] = jnp.zeros(l_ref.shape, F32)
    acc_ref[...] = jnp.zeros(acc_ref.shape, F32)

    def load_kv(head, start, size):
        rows = pl.ds(pl.multiple_of(start, size), size)
        k = jnp.concatenate([kn_ref[rows, head * MLA_NOPE:(head + 1) * MLA_NOPE], kpe_ref[rows, :]], axis=1)
        return k, v_ref[rows, head * MLA_V:(head + 1) * MLA_V]

    def update(head, half, k, v, diag_col=None):
        chain = 2 * head + half
        q = q_ref[half * th:(half + 1) * th, head * QK_PAD:(head + 1) * QK_PAD]
        s = lax.dot_general(q, k, (((1,), (1,)), ((), ())), preferred_element_type=F32)
        chunks = [s[:, c * LANES:(c + 1) * LANES] for c in range(s.shape[1] // LANES)]
        if diag_col is not None:
            row = lax.broadcasted_iota(jnp.int32, (th, LANES), 0)
            col = lax.broadcasted_iota(jnp.int32, (th, LANES), 1)
            for c in range(diag_col // LANES, len(chunks)):
                chunks[c] = jnp.where(col + (c * LANES - diag_col) <= row, chunks[c], NEG)
        mx = chunks[0]
        for c in chunks[1:]:
            mx = jnp.maximum(mx, c)
        m_prev = m_ref[chain]
        m_new = jnp.maximum(m_prev, jnp.max(mx, axis=-1, keepdims=True))
        alpha = jnp.exp2(m_prev - m_new)
        ps = [jnp.exp2(c - m_new) for c in chunks]
        psum = ps[0]
        for p in ps[1:]:
            psum = psum + p
        l_ref[chain] = alpha * l_ref[chain] + psum
        p = jnp.concatenate(ps, axis=1).astype(BF16)
        acc_ref[chain] = alpha * acc_ref[chain] + jnp.dot(p, v, preferred_element_type=F32)
        m_ref[chain] = m_new

    def body(j, carry):
        for head in range(heads):
            k, v = load_kv(head, j * tq, tq)
            update(head, 0, k, v)
            update(head, 1, k, v)
        return carry

    tq = 2 * th
    lax.fori_loop(0, qi, body, 0)
    for head in range(heads):
        k, v = load_kv(head, qi * tq, tq)
        update(head, 0, k[:th], v[:th], diag_col=0)
        update(head, 1, k, v, diag_col=th)
    for head in range(heads):
        for half in range(2):
            l = jnp.sum(l_ref[2 * head + half], axis=-1, keepdims=True)
            o_ref[half * th:(half + 1) * th, head * MLA_V:(head + 1) * MLA_V] = (
                acc_ref[2 * head + half] / l).astype(BF16)


def _mla_attention(q, kv, kpe, w_gate, w_up, B, S, th=512, heads=4):
    T = B * S
    tq = 2 * th
    nq = S // tq
    n_groups = MLA_HEADS // heads
    slab = D_MODEL // (B * n_groups * nq)
    w_spec = pl.BlockSpec((slab, D_FF), lambda b, h, i: ((b * n_groups + h) * nq + i, 0))
    return pl.pallas_call(
        functools.partial(_mla_kernel, th=th, heads=heads),
        grid=(B, n_groups, nq),
        in_specs=[
            pl.BlockSpec((tq, heads * QK_PAD), lambda b, h, i: (b * nq + i, h)),
            pl.BlockSpec((S, heads * MLA_NOPE), lambda b, h, i: (b, h)),
            pl.BlockSpec((S, LANES), lambda b, h, i: (b, 0)),
            pl.BlockSpec((S, heads * MLA_V), lambda b, h, i: (b, n_groups + h)),
            w_spec, w_spec,
        ],
        out_specs=[pl.BlockSpec((tq, heads * MLA_V), lambda b, h, i: (b * nq + i, h)), w_spec, w_spec],
        out_shape=[jax.ShapeDtypeStruct((T, MIX_A), BF16),
                   jax.ShapeDtypeStruct((D_MODEL, D_FF), BF16),
                   jax.ShapeDtypeStruct((D_MODEL, D_FF), BF16)],
        scratch_shapes=[
            pltpu.VMEM((2 * heads, th, LANES), F32),
            pltpu.VMEM((2 * heads, th, LANES), F32),
            pltpu.VMEM((2 * heads, th, MLA_V), F32),
        ],
        compiler_params=_params(("arbitrary", "arbitrary", "arbitrary")),
        name="mla_attn",
    )(q, kv, kpe, kv, w_gate, w_up)


def _swa_kernel(slope_ref, sink_ref, q0_ref, q1_ref, q2_ref, q3_ref, k_ref, kp_ref, v_ref, vp_ref,
                pc_ref, pr_ref, prp_ref, wd_ref, o_ref, wd_bf_ref, *, tq):
    wd_bf_ref[...] = wd_ref[...].astype(BF16)
    n = pl.program_id(1)
    pair = pl.program_id(2)
    q_refs = (q0_ref, q1_ref, q2_ref, q3_ref)
    lane = lax.broadcasted_iota(jnp.int32, (2 * BLOCK, LANES), 1)
    lo = lane < SWA_HEAD_DIM
    k_all = jnp.concatenate([kp_ref[...], k_ref[...]], axis=0)
    v_all = jnp.concatenate([vp_ref[...], v_ref[...]], axis=0)
    pos_k = jnp.concatenate([prp_ref[0], pr_ref[0]], axis=1)
    i_idx = lax.broadcasted_iota(jnp.int32, (BLOCK, 2 * BLOCK), 0)
    c_idx = lax.broadcasted_iota(jnp.int32, (BLOCK, 2 * BLOCK), 1)
    in_window = (c_idx > i_idx) & (c_idx <= i_idx + SWA_WINDOW)
    zero = jnp.zeros((), BF16)
    lo_out = lax.broadcasted_iota(jnp.int32, (BLOCK, LANES), 1) < SWA_HEAD_DIM

    for t in range(tq // BLOCK):
        r0 = t * BLOCK
        band_k = k_all[r0:r0 + 2 * BLOCK]
        band_v = v_all[r0:r0 + 2 * BLOCK]
        k_bd = jnp.concatenate([jnp.where(lo, band_k, zero), jnp.where(lo, zero, band_k)], axis=0)
        v_bd = jnp.concatenate([jnp.where(lo, band_v, zero), jnp.where(lo, zero, band_v)], axis=0)
        q_st = jnp.concatenate([qr[r0:r0 + BLOCK, :] for qr in q_refs], axis=0)
        s = lax.dot_general(q_st, k_bd, (((1,), (1,)), ((), ())), preferred_element_type=F32)
        dist = jnp.abs(pc_ref[r0:r0 + BLOCK, :] - pos_k[:, r0:r0 + 2 * BLOCK]).astype(F32)
        valid = in_window
        if t == 0:
            valid = valid & ((c_idx >= BLOCK) | (n > 0))
        dist = jnp.where(valid, dist, MASKED_DIST)
        rows = []
        inv_den = []
        for g in range(SWA_GROUP):
            halves = []
            for c in range(2):
                head = (2 * pair + c) * SWA_GROUP + g
                slope = slope_ref[head] * LOG2_E
                sink = sink_ref[head] * LOG2_E
                sg = s[g * BLOCK:(g + 1) * BLOCK, c * 2 * BLOCK:(c + 1) * 2 * BLOCK] - slope * dist
                m = jnp.maximum(jnp.max(sg, axis=-1, keepdims=True), sink)
                e = jnp.exp2(sg - m)
                den = jnp.sum(e, axis=-1, keepdims=True) + jnp.exp2(sink - m)
                halves.append(e.astype(BF16))
                inv_den.append(1.0 / den)
            rows.append(jnp.concatenate(halves, axis=1))
        p = jnp.concatenate(rows, axis=0)
        out = jnp.dot(p, v_bd, preferred_element_type=F32)
        for g in range(SWA_GROUP):
            norm = jnp.where(lo_out, inv_den[2 * g], inv_den[2 * g + 1])
            o_ref[g, r0:r0 + BLOCK, :] = (out[g * BLOCK:(g + 1) * BLOCK] * norm).astype(BF16)


def _swa_attention(proj, pos_col, pos_row, slopes, sinks, w_down, B, S, tq=1024):
    T = B * S
    nt = S // tq
    sub = tq // BLOCK
    npairs = SWA_KV_HEADS // 2
    n_steps = B * nt * npairs
    slab = FF_TILE if n_steps >= D_FF // FF_TILE else D_FF
    last_slab = D_FF // slab - 1
    wd_spec = pl.BlockSpec(
        (slab, D_MODEL), lambda b, n, j: (jnp.minimum((b * nt + n) * npairs + j, last_slab), 0))
    qs0 = COL_QS // LANES
    ks0 = COL_KS // LANES
    vs0 = COL_VS // LANES
    pairs_per_group = KV_W // LANES

    def q_spec(g):
        return pl.BlockSpec((tq, LANES), lambda b, n, j, g=g: (b * nt + n, qs0 + pairs_per_group * g + j))

    def prev_row(b, n):
        return jnp.maximum(b * (S // BLOCK) + sub * n - 1, 0)

    smem = pl.BlockSpec(memory_space=pltpu.SMEM)
    return pl.pallas_call(
        functools.partial(_swa_kernel, tq=tq),
        grid=(B, nt, npairs),
        in_specs=[
            smem, smem,
            q_spec(0), q_spec(1), q_spec(2), q_spec(3),
            pl.BlockSpec((tq, LANES), lambda b, n, j: (b * nt + n, ks0 + j)),
            pl.BlockSpec((BLOCK, LANES), lambda b, n, j: (prev_row(b, n), ks0 + j)),
            pl.BlockSpec((tq, LANES), lambda b, n, j: (b * nt + n, vs0 + j)),
            pl.BlockSpec((BLOCK, LANES), lambda b, n, j: (prev_row(b, n), vs0 + j)),
            pl.BlockSpec((tq, 1), lambda b, n, j: (b * nt + n, 0)),
            pl.BlockSpec((1, 1, tq), lambda b, n, j: (b, 0, n)),
            pl.BlockSpec((1, 1, BLOCK), lambda b, n, j: (b, 0, jnp.maximum(sub * n - 1, 0))),
            wd_spec,
        ],
        out_specs=[pl.BlockSpec((SWA_GROUP, tq, LANES), lambda b, n, j: (0, b * nt + n, j)), wd_spec],
        out_shape=[jax.ShapeDtypeStruct((SWA_GROUP, T, KV_W), BF16),
                   jax.ShapeDtypeStruct((D_FF, D_MODEL), BF16)],
        compiler_params=_params(("arbitrary", "arbitrary", "arbitrary")),
        name="swa_attn",
    )(slopes, sinks, proj, proj, proj, proj, proj, proj, proj, proj, pos_col, pos_row, pos_row, w_down)


def _oproj_a_kernel(oa_ref, ga_ref, w_ref, y_ref):
    oa = oa_ref[...].astype(F32)
    r = _row_rsqrt(oa)
    og = (oa * ga_ref[...]).astype(BF16)
    for cs in _col_chunks(D_MODEL):
        y_ref[:, cs] = jnp.dot(og, w_ref[:, cs], preferred_element_type=F32) * r


def _out_proj_a(o_a, ga, w_oa, tm=512):
    T = o_a.shape[0]
    row = lambda i: (i, 0)
    return pl.pallas_call(
        _oproj_a_kernel,
        grid=(T // tm,),
        in_specs=[pl.BlockSpec((tm, MIX_A), row), _resident((1, MIX_A)), _resident((MIX_A, D_MODEL))],
        out_specs=pl.BlockSpec((tm, D_MODEL), row),
        out_shape=jax.ShapeDtypeStruct((T, D_MODEL), F32),
        compiler_params=_params(("parallel",)),
        name="out_proj_a",
    )(o_a, ga, w_oa)


def _oproj_b_kernel(ob_ref, gb_ref, w_ref, ya_ref, x_ref, gp_ref, h_ref):
    ob = jnp.concatenate([ob_ref[g] for g in range(SWA_GROUP)], axis=1).astype(F32)
    r = _row_rsqrt(ob)
    og = (ob * gb_ref[...]).astype(BF16)
    ssq = jnp.zeros((h_ref.shape[0], 1), F32)
    for cs in _col_chunks(D_MODEL):
        y = ya_ref[:, cs] + jnp.dot(og, w_ref[:, cs], preferred_element_type=F32) * r
        ssq = ssq + jnp.sum(y * y, axis=-1, keepdims=True)
        h_ref[:, cs] = y
    rn = lax.rsqrt(ssq * (1.0 / D_MODEL) + EPS)
    for cs in _col_chunks(D_MODEL):
        h_ref[:, cs] = x_ref[:, cs] + h_ref[:, cs] * rn * gp_ref[:, cs]


def _out_proj_b(o_b, gb, w_o_bf, y_a, x2, gpost, tm=256):
    T = x2.shape[0]
    row = lambda i: (i, 0)
    return pl.pallas_call(
        _oproj_b_kernel,
        grid=(T // tm,),
        in_specs=[
            pl.BlockSpec((SWA_GROUP, tm, KV_W), lambda i: (0, i, 0)),
            _resident((1, MIX_B)),
            _resident((MIX_B, D_MODEL), block=(MIX_A // MIX_B, 0)),
            pl.BlockSpec((tm, D_MODEL), row),
            pl.BlockSpec((tm, D_MODEL), row),
            _resident((1, D_MODEL)),
        ],
        out_specs=pl.BlockSpec((tm, D_MODEL), row),
        out_shape=jax.ShapeDtypeStruct((T, D_MODEL), F32),
        compiler_params=_params(("parallel",)),
        name="out_proj_b",
    )(o_b, gb, w_o_bf, y_a, x2, gpost)


def _ffn_kernel(h_ref, gpre_ref, wg_ref, wu_ref, wd_ref, gpost_ref, o_ref, f_ref, *, nj):
    j = pl.program_id(1)

    @pl.when(j == 0)
    def _():
        def norm_rows(rows):
            f_ref[rows, :] = _rms(h_ref[rows, :], gpre_ref[...]).astype(BF16)
            o_ref[rows, :] = jnp.zeros((FINISH_ROWS, D_MODEL), F32)
        _for_row_chunks(f_ref.shape[0], norm_rows, FINISH_ROWS)

    f = f_ref[...]
    gate = jnp.dot(f, wg_ref[...], preferred_element_type=F32)
    up = jnp.dot(f, wu_ref[...], preferred_element_type=F32)
    act = (gate * jax.nn.sigmoid(gate) * up).astype(BF16)
    for cs in _col_chunks(D_MODEL, DOWN_CHUNK):
        o_ref[:, cs] += jnp.dot(act, wd_ref[:, cs], preferred_element_type=F32)

    @pl.when(j == nj - 1)
    def _():
        def finish_rows(rows):
            o_ref[rows, :] = h_ref[rows, :] + _rms(o_ref[rows, :], gpost_ref[...])
        _for_row_chunks(o_ref.shape[0], finish_rows, FINISH_ROWS)


def _ffn(h, gpre, wg, wu, wd, gpost, tm=512, tf=FF_TILE):
    T = h.shape[0]
    nj = D_FF // tf
    return pl.pallas_call(
        functools.partial(_ffn_kernel, nj=nj),
        grid=(T // tm, nj),
        in_specs=[
            pl.BlockSpec((tm, D_MODEL), lambda i, j: (i, 0)),
            pl.BlockSpec((1, D_MODEL), lambda i, j: (0, 0)),
            pl.BlockSpec((D_MODEL, tf), lambda i, j: (0, j)),
            pl.BlockSpec((D_MODEL, tf), lambda i, j: (0, j)),
            pl.BlockSpec((tf, D_MODEL), lambda i, j: (j, 0)),
            pl.BlockSpec((1, D_MODEL), lambda i, j: (0, 0)),
        ],
        out_specs=pl.BlockSpec((tm, D_MODEL), lambda i, j: (i, 0)),
        out_shape=jax.ShapeDtypeStruct((T, D_MODEL), F32),
        scratch_shapes=[pltpu.VMEM((tm, D_MODEL), BF16)],
        compiler_params=_params(("parallel", "arbitrary")),
        name="ffn",
    )(h, gpre, wg, wu, wd, gpost)


def _w_in_prep_kernel(w_ref, lat_ref, rope_ref, q_ref, kv_ref):
    c0 = LATENT_W
    c1 = c0 + MLA_ROPE
    c2 = c1 + MIX_B
    half = MLA_ROPE // 2
    w = w_ref[...]
    lat_ref[...] = w[:, :c0].astype(BF16)
    rope_ref[...] = jnp.concatenate([w[:, c0:c1], w[:, c0 + half:c1], w[:, c0:c0 + half]], axis=1).astype(BF16)
    for g in range(SWA_GROUP):
        for pair in range(SWA_KV_HEADS // 2):
            srcs = [c1 + ((2 * pair + c) * SWA_GROUP + g) * SWA_HEAD_DIM for c in range(2)]
            dst = (g * SWA_KV_HEADS + 2 * pair) * SWA_HEAD_DIM
            q_ref[:, dst:dst + LANES] = jnp.concatenate(
                [w[:, a:a + SWA_HEAD_DIM] for a in srcs], axis=1).astype(BF16)
    kv_ref[...] = w[:, c2:].astype(BF16)


def _prep_w_in(w_in, layer, rows=128):
    row = lambda i: (i, 0)
    widths = (LATENT_W, LANES, MIX_B, 2 * KV_W)
    return pl.pallas_call(
        _w_in_prep_kernel,
        grid=(D_MODEL // rows,),
        in_specs=[pl.BlockSpec((None, rows, w_in.shape[2]), lambda i: (layer, i, 0))],
        out_specs=[pl.BlockSpec((rows, n), row) for n in widths],
        out_shape=[jax.ShapeDtypeStruct((D_MODEL, n), BF16) for n in widths],
        compiler_params=_params(("parallel",)),
        name="w_in_prep",
    )(w_in)


def _w_up_prep_kernel(wq_ref, wkv_ref, wq_o_ref, wkv_o_ref):
    half = MLA_ROPE // 2
    wq = wq_ref[...]
    wkv = wkv_ref[...]
    for h in range(MLA_HEADS):
        src = h * MLA_QK
        pe = src + MLA_NOPE
        wq_o_ref[:, h * QK_PAD:(h + 1) * QK_PAD] = jnp.concatenate(
            [wq[:, src:src + MLA_QK], wq[:, pe + half:pe + MLA_ROPE], wq[:, pe:pe + half]], axis=1).astype(BF16)
        kv_src = h * (MLA_NOPE + MLA_V)
        wkv_o_ref[:, h * MLA_NOPE:(h + 1) * MLA_NOPE] = wkv[:, kv_src:kv_src + MLA_NOPE].astype(BF16)
        wkv_o_ref[:, MIX_A + h * MLA_V:MIX_A + (h + 1) * MLA_V] = (
            wkv[:, kv_src + MLA_NOPE:kv_src + MLA_NOPE + MLA_V].astype(BF16))


def _prep_w_up(w_uq, w_ukv, steps=8):
    row = lambda i: (i, 0)
    rq, rkv = MLA_Q_RANK // steps, MLA_KV_RANK // steps
    nq, nkv = MLA_HEADS * QK_PAD, MLA_HEADS * (MLA_NOPE + MLA_V)
    return pl.pallas_call(
        _w_up_prep_kernel,
        grid=(steps,),
        in_specs=[pl.BlockSpec((rq, w_uq.shape[1]), row), pl.BlockSpec((rkv, w_ukv.shape[1]), row)],
        out_specs=[pl.BlockSpec((rq, nq), row), pl.BlockSpec((rkv, nkv), row)],
        out_shape=[jax.ShapeDtypeStruct((MLA_Q_RANK, nq), BF16), jax.ShapeDtypeStruct((MLA_KV_RANK, nkv), BF16)],
        compiler_params=_params(("parallel",)),
        name="w_up_prep",
    )(w_uq, w_ukv)


def _swa_perm_rows(a):
    rest = a.shape[1:]
    a = a.reshape((SWA_KV_HEADS, SWA_GROUP, SWA_HEAD_DIM) + rest)
    return jnp.swapaxes(a, 0, 1).reshape((MIX_B,) + rest)


def kernel(x, positions, attn_pre_g, w_in, q_norm_g, w_uq, kv_norm_g, w_ukv, swa_sinks, grp_a_g, grp_b_g,
           w_o, attn_post_g, ffn_pre_g, w_gate, w_up, w_down, ffn_post_g):
    B, S, _ = x.shape
    T = B * S
    depth = w_in.shape[0]
    inv = 1.0 / (ROPE_THETA ** (jnp.arange(0, MLA_ROPE, 2, dtype=F32) / MLA_ROPE))
    zeros64 = jnp.zeros((MLA_ROPE,), F32)
    inv_l = jnp.concatenate([inv, inv, zeros64]).reshape(1, LANES)
    cmask = jnp.concatenate([jnp.ones((MLA_ROPE,), F32), zeros64]).reshape(1, LANES)
    half = MLA_ROPE // 2
    sgn = jnp.concatenate([-jnp.ones((half,), F32), jnp.ones((half,), F32), zeros64]).reshape(1, LANES)
    slopes = jnp.exp2(-8.0 * jnp.arange(1, SWA_Q_HEADS + 1, dtype=F32) / SWA_Q_HEADS)
    pos_col = positions.reshape(T, 1)
    pos_row = positions.reshape(B, 1, S)

    h = x.reshape(T, D_MODEL)
    for l in range(depth):
        w_lat, w_rope, w_q, w_kv = _prep_w_in(w_in, l)
        w_uq_r, w_ukv_r = _prep_w_up(w_uq[l], w_ukv[l])
        gb = _swa_perm_rows(grp_b_g[l]).reshape(1, MIX_B)
        g_pre = attn_pre_g[l].reshape(1, D_MODEL)

        lat, kr = _in_proj_mla(h, g_pre, w_lat, w_rope)
        proj_swa, w_o_bf = _in_proj_swa(h, g_pre, w_q, w_kv, w_o[l])
        q, kv, kpe = _qkv_up(lat, kr, pos_col, inv_l, cmask, sgn, q_norm_g[l].reshape(1, MLA_Q_RANK),
                             kv_norm_g[l].reshape(1, MLA_KV_RANK), w_uq_r, w_ukv_r)
        o_a, wg_bf, wu_bf = _mla_attention(q, kv, kpe, w_gate[l], w_up[l], B, S)
        y_a = _out_proj_a(o_a, grp_a_g[l].reshape(1, MIX_A), w_o_bf)
        o_b, wd_bf = _swa_attention(proj_swa, pos_col, pos_row, slopes, swa_sinks[l].astype(F32),
                                    w_down[l], B, S)
        h = _out_proj_b(o_b, gb, w_o_bf, y_a, h, attn_post_g[l].reshape(1, D_MODEL))
        h = _ffn(h, ffn_pre_g[l].reshape(1, D_MODEL), wg_bf, wu_bf, wd_bf, ffn_post_g[l].reshape(1, D_MODEL))
    return h.reshape(B, S, D_MODEL)
```

```python
import functools

import jax
import jax.numpy as jnp
from jax import lax
from jax.experimental import pallas as pl
from jax.experimental.pallas import tpu as pltpu

D_MODEL = 4096
MLA_HEADS = 16
MLA_Q_RANK = 1024
MLA_KV_RANK = 512
MLA_NOPE = 128
MLA_ROPE = 64
MLA_V = 128
MLA_QK = MLA_NOPE + MLA_ROPE
ROPE_THETA = 10000.0
SWA_Q_HEADS = 32
SWA_KV_HEADS = 8
SWA_HEAD_DIM = 64
SWA_GROUP = SWA_Q_HEADS // SWA_KV_HEADS
SWA_WINDOW = 128
BLOCK = 128
MIX_A = MLA_HEADS * MLA_V
MIX_B = SWA_Q_HEADS * SWA_HEAD_DIM
D_FF = 11008
EPS = 1e-6

LANES = 128
QK_PAD = 256
KV_W = SWA_KV_HEADS * SWA_HEAD_DIM
LATENT_W = MLA_Q_RANK + MLA_KV_RANK
COL_QS = 0
COL_KS = COL_QS + MIX_B
COL_VS = COL_KS + KV_W
SWA_IN_W = COL_VS + KV_W
NEG = float(jnp.finfo(jnp.float32).min)
VMEM_LIMIT = 56 * 1024 * 1024
DOWN_CHUNK = 1024
FF_TILE = 256
N_CHUNK = 1024
FINISH_ROWS = 64
LOG2_E = 1.4426950408889634
MASKED_DIST = 1e30
F32 = jnp.float32
BF16 = jnp.bfloat16


def _rms(xf, g):
    return xf * lax.rsqrt(jnp.mean(xf * xf, axis=-1, keepdims=True) + EPS) * g


def _row_rsqrt(xf):
    return lax.rsqrt(jnp.mean(xf * xf, axis=-1, keepdims=True) + EPS)


def _for_row_chunks(n_rows, fn, chunk, unroll=1):
    def body(r, carry):
        fn(pl.ds(pl.multiple_of(r * chunk, chunk), chunk))
        return carry
    lax.fori_loop(0, n_rows // chunk, body, 0, unroll=unroll)


def _params(sem):
    return pltpu.CompilerParams(dimension_semantics=sem, vmem_limit_bytes=VMEM_LIMIT)


def _resident(shape, block=None):
    block = (0,) * len(shape) if block is None else block
    return pl.BlockSpec(shape, lambda *_: block, pipeline_mode=pl.Buffered(1))


def _col_chunks(width, chunk=N_CHUNK):
    return [slice(lo, min(lo + chunk, width)) for lo in range(0, width, chunk)]


def _inproj_mla_kernel(x_ref, g_ref, wl_ref, wr_ref, lat_ref, kr_ref):
    x = x_ref[...]
    r = _row_rsqrt(x)
    xg = (x * g_ref[...]).astype(BF16)
    lat_ref[...] = (jnp.dot(xg, wl_ref[...], preferred_element_type=F32) * r).astype(BF16)
    kr_ref[...] = jnp.dot(xg, wr_ref[...], preferred_element_type=F32) * r


def _in_proj_mla(x2, g, w_lat, w_rope, tm=512):
    T = x2.shape[0]
    row = lambda i: (i, 0)
    return pl.pallas_call(
        _inproj_mla_kernel,
        grid=(T // tm,),
        in_specs=[
            pl.BlockSpec((tm, D_MODEL), row),
            _resident((1, D_MODEL)),
            _resident((D_MODEL, LATENT_W)),
            _resident((D_MODEL, LANES)),
        ],
        out_specs=[pl.BlockSpec((tm, LATENT_W), row), pl.BlockSpec((tm, LANES), row)],
        out_shape=[jax.ShapeDtypeStruct((T, LATENT_W), BF16), jax.ShapeDtypeStruct((T, LANES), F32)],
        compiler_params=_params(("parallel",)),
        name="in_proj_mla",
    )(x2, g, w_lat, w_rope)


def _inproj_swa_kernel(x_ref, g_ref, wq_ref, wkv_ref, wo_ref, o_ref, wo_bf_ref):
    wo_bf_ref[...] = wo_ref[...].astype(BF16)
    x = x_ref[...]
    r = _row_rsqrt(x)
    xg = (x * g_ref[...]).astype(BF16)
    rq = r * (SWA_HEAD_DIM ** -0.5 * LOG2_E)
    for cs in _col_chunks(MIX_B):
        o_ref[:, cs] = (jnp.dot(xg, wq_ref[:, cs], preferred_element_type=F32) * rq).astype(BF16)
    o_ref[:, COL_KS:] = (jnp.dot(xg, wkv_ref[...], preferred_element_type=F32) * r).astype(BF16)


def _in_proj_swa(x2, g, w_q, w_kv, w_o):
    T = x2.shape[0]
    n_slabs = (MIX_A + MIX_B) // SWA_HEAD_DIM
    n_top = MIX_A // SWA_HEAD_DIM
    tm = T // n_slabs
    row = lambda i: (i, 0)

    def slab_dst(i):
        r = jnp.maximum(i - n_top, 0)
        return jnp.where(i < n_top, i, n_top + (r % SWA_GROUP) * SWA_KV_HEADS + r // SWA_GROUP), 0

    return pl.pallas_call(
        _inproj_swa_kernel,
        grid=(n_slabs,),
        in_specs=[
            pl.BlockSpec((tm, D_MODEL), row),
            _resident((1, D_MODEL)),
            _resident((D_MODEL, MIX_B)),
            _resident((D_MODEL, 2 * KV_W)),
            pl.BlockSpec((SWA_HEAD_DIM, D_MODEL), row),
        ],
        out_specs=[pl.BlockSpec((tm, SWA_IN_W), row), pl.BlockSpec((SWA_HEAD_DIM, D_MODEL), slab_dst)],
        out_shape=[jax.ShapeDtypeStruct((T, SWA_IN_W), BF16),
                   jax.ShapeDtypeStruct((MIX_A + MIX_B, D_MODEL), BF16)],
        compiler_params=_params(("arbitrary",)),
        name="in_proj_swa",
    )(x2, g, w_q, w_kv, w_o)


def _qkvup_kernel(cq_ref, ckv_ref, kr_ref, pos_ref, inv_ref, cmask_ref, sgn_ref, gq_ref, gkv_ref,
                  wq_ref, wkv_ref, q_ref, kv_ref, kpe_ref):
    ang = pos_ref[...].astype(F32) * inv_ref[...]
    c = jnp.cos(ang) * cmask_ref[...]
    s = jnp.sin(ang) * sgn_ref[...]
    kr = kr_ref[...]
    kpe_ref[...] = (kr * c + pltpu.roll(kr, 64, 1) * s).astype(BF16)
    scale = MLA_QK ** -0.5 * LOG2_E
    cq = cq_ref[...].astype(F32)
    rq = _row_rsqrt(cq) * scale
    cqg = (cq * gq_ref[...]).astype(BF16)
    for cs in _col_chunks(MLA_HEADS * QK_PAD):
        y = jnp.dot(cqg, wq_ref[:, cs], preferred_element_type=F32) * rq
        for h in range((cs.stop - cs.start) // QK_PAD):
            lo = h * QK_PAD
            q_ref[:, cs.start + lo:cs.start + lo + MLA_NOPE] = y[:, lo:lo + MLA_NOPE].astype(BF16)
            pe = y[:, lo + MLA_NOPE:lo + QK_PAD]
            q_ref[:, cs.start + lo + MLA_NOPE:cs.start + lo + QK_PAD] = (
                pe * c + pltpu.roll(pe, 64, 1) * s).astype(BF16)
    ckv = ckv_ref[...].astype(F32)
    rkv = _row_rsqrt(ckv)
    ckvg = (ckv * gkv_ref[...]).astype(BF16)
    for cs in _col_chunks(MLA_HEADS * (MLA_NOPE + MLA_V)):
        kv_ref[:, cs] = (jnp.dot(ckvg, wkv_ref[:, cs], preferred_element_type=F32) * rkv).astype(BF16)


def _qkv_up(lat, kr, pos_col, inv, cmask, sgn, gq, gkv, w_uq_r, w_ukv_r, tm=512):
    T = lat.shape[0]
    nq = MLA_HEADS * QK_PAD
    nkv = MLA_HEADS * (MLA_NOPE + MLA_V)
    row = lambda i: (i, 0)
    return pl.pallas_call(
        _qkvup_kernel,
        grid=(T // tm,),
        in_specs=[
            pl.BlockSpec((tm, MLA_Q_RANK), row),
            pl.BlockSpec((tm, MLA_KV_RANK), lambda i: (i, MLA_Q_RANK // MLA_KV_RANK)),
            pl.BlockSpec((tm, LANES), row),
            pl.BlockSpec((tm, 1), row),
            _resident((1, LANES)),
            _resident((1, LANES)),
            _resident((1, LANES)),
            _resident((1, MLA_Q_RANK)),
            _resident((1, MLA_KV_RANK)),
            _resident((MLA_Q_RANK, nq)),
            _resident((MLA_KV_RANK, nkv)),
        ],
        out_specs=[pl.BlockSpec((tm, nq), row), pl.BlockSpec((tm, nkv), row), pl.BlockSpec((tm, LANES), row)],
        out_shape=[jax.ShapeDtypeStruct((T, nq), BF16), jax.ShapeDtypeStruct((T, nkv), BF16),
                   jax.ShapeDtypeStruct((T, LANES), BF16)],
        compiler_params=_params(("parallel",)),
        name="qkv_up",
    )(lat, lat, kr, pos_col, inv, cmask, sgn, gq, gkv, w_uq_r, w_ukv_r)


def _mla_kernel(q_ref, kn_ref, kpe_ref, v_ref, wg_ref, wu_ref, o_ref, wg_bf_ref, wu_bf_ref,
                m_ref, l_ref, acc_ref, *, th, heads):
    wg_bf_ref[...] = wg_ref[...].astype(BF16)
    wu_bf_ref[...] = wu_ref[...].astype(BF16)
    qi = pl.program_id(2)
    m_ref[...] = jnp.full(m_ref.shape, NEG, F32)
    l_ref[...] = jnp.zeros(l_ref.shape, F32)
    acc_ref[...] = jnp.zeros(acc_ref.shape, F32)

    def load_kv(head, start, size):
        rows = pl.ds(pl.multiple_of(start, size), size)
        k = jnp.concatenate([kn_ref[rows, head * MLA_NOPE:(head + 1) * MLA_NOPE], kpe_ref[rows, :]], axis=1)
        return k, v_ref[rows, head * MLA_V:(head + 1) * MLA_V]

    def update(head, half, k, v, diag_col=None):
        if half is None:
            rows, slots, n = slice(0, 2 * th), slice(2 * head, 2 * head + 2), 2 * th
        else:
            rows, slots, n = slice(half * th, (half + 1) * th), slice(2 * head + half, 2 * head + half + 1), th
        q = q_ref[rows, head * QK_PAD:(head + 1) * QK_PAD]
        s = lax.dot_general(q, k, (((1,), (1,)), ((), ())), preferred_element_type=F32)
        chunks = [s[:, c * LANES:(c + 1) * LANES] for c in range(s.shape[1] // LANES)]
        if diag_col is not None:
            row = lax.broadcasted_iota(jnp.int32, (n, LANES), 0)
            col = lax.broadcasted_iota(jnp.int32, (n, LANES), 1)
            for c in range(diag_col // LANES, len(chunks)):
                chunks[c] = jnp.where(col + (c * LANES - diag_col) <= row, chunks[c], NEG)
        mx = chunks[0]
        for c in chunks[1:]:
            mx = jnp.maximum(mx, c)
        m_prev = m_ref[slots].reshape(n, LANES)
        m_new = jnp.maximum(m_prev, jnp.max(mx, axis=-1, keepdims=True))
        alpha = jnp.exp2(m_prev - m_new)
        ps = [jnp.exp2(c - m_new) for c in chunks]
        psum = ps[0]
        for p in ps[1:]:
            psum = psum + p
        l_ref[slots] = (alpha * l_ref[slots].reshape(n, LANES) + psum).reshape(-1, th, LANES)
        p = jnp.concatenate(ps, axis=1).astype(BF16)
        acc = alpha * acc_ref[slots].reshape(n, MLA_V) + jnp.dot(p, v, preferred_element_type=F32)
        acc_ref[slots] = acc.reshape(-1, th, MLA_V)
        m_ref[slots] = m_new.reshape(-1, th, LANES)

    def body(j, carry):
        for head in range(heads):
            k, v = load_kv(head, j * tq, tq)
            update(head, None, k, v)
        return carry

    tq = 2 * th
    lax.fori_loop(0, qi, body, 0)
    for head in range(heads):
        k, v = load_kv(head, qi * tq, tq)
        update(head, 0, k[:th], v[:th], diag_col=0)
        update(head, 1, k, v, diag_col=th)
    for head in range(heads):
        for half in range(2):
            l = jnp.sum(l_ref[2 * head + half], axis=-1, keepdims=True)
            o_ref[half * th:(half + 1) * th, head * MLA_V:(head + 1) * MLA_V] = (
                acc_ref[2 * head + half] / l).astype(BF16)


def _mla_attention(q, kv, kpe, w_gate, w_up, B, S, th=512, heads=4):
    T = B * S
    tq = 2 * th
    nq = S // tq
    n_groups = MLA_HEADS // heads
    slab = D_MODEL // (B * n_groups * nq)
    w_spec = pl.BlockSpec((slab, D_FF), lambda b, h, i: ((b * n_groups + h) * nq + i, 0))
    return pl.pallas_call(
        functools.partial(_mla_kernel, th=th, heads=heads),
        grid=(B, n_groups, nq),
        in_specs=[
            pl.BlockSpec((tq, heads * QK_PAD), lambda b, h, i: (b * nq + i, h)),
            pl.BlockSpec((S, heads * MLA_NOPE), lambda b, h, i: (b, h)),
            pl.BlockSpec((S, LANES), lambda b, h, i: (b, 0)),
            pl.BlockSpec((S, heads * MLA_V), lambda b, h, i: (b, n_groups + h)),
            w_spec, w_spec,
        ],
        out_specs=[pl.BlockSpec((tq, heads * MLA_V), lambda b, h, i: (b * nq + i, h)), w_spec, w_spec],
        out_shape=[jax.ShapeDtypeStruct((T, MIX_A), BF16),
                   jax.ShapeDtypeStruct((D_MODEL, D_FF), BF16),
                   jax.ShapeDtypeStruct((D_MODEL, D_FF), BF16)],
        scratch_shapes=[
            pltpu.VMEM((2 * heads, th, LANES), F32),
            pltpu.VMEM((2 * heads, th, LANES), F32),
            pltpu.VMEM((2 * heads, th, MLA_V), F32),
        ],
        compiler_params=_params(("arbitrary", "arbitrary", "arbitrary")),
        name="mla_attn",
    )(q, kv, kpe, kv, w_gate, w_up)


def _swa_kernel(slope_ref, sink_ref, q0_ref, q1_ref, q2_ref, q3_ref, k_ref, kp_ref, v_ref, vp_ref,
                pc_ref, pr_ref, prp_ref, wd_ref, o_ref, wd_bf_ref, *, tq):
    wd_bf_ref[...] = wd_ref[...].astype(BF16)
    n = pl.program_id(1)
    pair = pl.program_id(2)
    q_refs = (q0_ref, q1_ref, q2_ref, q3_ref)
    lane = lax.broadcasted_iota(jnp.int32, (2 * BLOCK, LANES), 1)
    lo = lane < SWA_HEAD_DIM
    k_all = jnp.concatenate([kp_ref[...], k_ref[...]], axis=0)
    v_all = jnp.concatenate([vp_ref[...], v_ref[...]], axis=0)
    pos_k = jnp.concatenate([prp_ref[0], pr_ref[0]], axis=1)
    i_idx = lax.broadcasted_iota(jnp.int32, (BLOCK, 2 * BLOCK), 0)
    c_idx = lax.broadcasted_iota(jnp.int32, (BLOCK, 2 * BLOCK), 1)
    in_window = (c_idx > i_idx) & (c_idx <= i_idx + SWA_WINDOW)
    zero = jnp.zeros((), BF16)
    lo_out = lax.broadcasted_iota(jnp.int32, (BLOCK, LANES), 1) < SWA_HEAD_DIM

    for t in range(tq // BLOCK):
        r0 = t * BLOCK
        band_k = k_all[r0:r0 + 2 * BLOCK]
        band_v = v_all[r0:r0 + 2 * BLOCK]
        k_bd = jnp.concatenate([jnp.where(lo, band_k, zero), jnp.where(lo, zero, band_k)], axis=0)
        v_bd = jnp.concatenate([jnp.where(lo, band_v, zero), jnp.where(lo, zero, band_v)], axis=0)
        q_st = jnp.concatenate([qr[r0:r0 + BLOCK, :] for qr in q_refs], axis=0)
        s = lax.dot_general(q_st, k_bd, (((1,), (1,)), ((), ())), preferred_element_type=F32)
        dist = jnp.abs(pc_ref[r0:r0 + BLOCK, :] - pos_k[:, r0:r0 + 2 * BLOCK]).astype(F32)
        valid = in_window
        if t == 0:
            valid = valid & ((c_idx >= BLOCK) | (n > 0))
        dist = jnp.where(valid, dist, MASKED_DIST)
        rows = []
        inv_den = []
        for g in range(SWA_GROUP):
            halves = []
            for c in range(2):
                head = (2 * pair + c) * SWA_GROUP + g
                slope = slope_ref[head] * LOG2_E
                sink = sink_ref[head] * LOG2_E
                sg = s[g * BLOCK:(g + 1) * BLOCK, c * 2 * BLOCK:(c + 1) * 2 * BLOCK] - slope * dist
                m = jnp.maximum(jnp.max(sg, axis=-1, keepdims=True), sink)
                e = jnp.exp2(sg - m)
                den = jnp.sum(e, axis=-1, keepdims=True) + jnp.exp2(sink - m)
                halves.append(e.astype(BF16))
                inv_den.append(1.0 / den)
            rows.append(jnp.concatenate(halves, axis=1))
        p = jnp.concatenate(rows, axis=0)
        out = jnp.dot(p, v_bd, preferred_element_type=F32)
        for g in range(SWA_GROUP):
            norm = jnp.where(lo_out, inv_den[2 * g], inv_den[2 * g + 1])
            o_ref[g, r0:r0 + BLOCK, :] = (out[g * BLOCK:(g + 1) * BLOCK] * norm).astype(BF16)


def _swa_attention(proj, pos_col, pos_row, slopes, sinks, w_down, B, S, tq=1024):
    T = B * S
    nt = S // tq
    sub = tq // BLOCK
    npairs = SWA_KV_HEADS // 2
    n_steps = B * nt * npairs
    slab = FF_TILE if n_steps >= D_FF // FF_TILE else D_FF
    last_slab = D_FF // slab - 1
    wd_spec = pl.BlockSpec(
        (slab, D_MODEL), lambda b, n, j: (jnp.minimum((b * nt + n) * npairs + j, last_slab), 0))
    qs0 = COL_QS // LANES
    ks0 = COL_KS // LANES
    vs0 = COL_VS // LANES
    pairs_per_group = KV_W // LANES

    def q_spec(g):
        return pl.BlockSpec((tq, LANES), lambda b, n, j, g=g: (b * nt + n, qs0 + pairs_per_group * g + j))

    def prev_row(b, n):
        return jnp.maximum(b * (S // BLOCK) + sub * n - 1, 0)

    smem = pl.BlockSpec(memory_space=pltpu.SMEM)
    return pl.pallas_call(
        functools.partial(_swa_kernel, tq=tq),
        grid=(B, nt, npairs),
        in_specs=[
            smem, smem,
            q_spec(0), q_spec(1), q_spec(2), q_spec(3),
            pl.BlockSpec((tq, LANES), lambda b, n, j: (b * nt + n, ks0 + j)),
            pl.BlockSpec((BLOCK, LANES), lambda b, n, j: (prev_row(b, n), ks0 + j)),
            pl.BlockSpec((tq, LANES), lambda b, n, j: (b * nt + n, vs0 + j)),
            pl.BlockSpec((BLOCK, LANES), lambda b, n, j: (prev_row(b, n), vs0 + j)),
            pl.BlockSpec((tq, 1), lambda b, n, j: (b * nt + n, 0)),
            pl.BlockSpec((1, 1, tq), lambda b, n, j: (b, 0, n)),
            pl.BlockSpec((1, 1, BLOCK), lambda b, n, j: (b, 0, jnp.maximum(sub * n - 1, 0))),
            wd_spec,
        ],
        out_specs=[pl.BlockSpec((SWA_GROUP, tq, LANES), lambda b, n, j: (0, b * nt + n, j)), wd_spec],
        out_shape=[jax.ShapeDtypeStruct((SWA_GROUP, T, KV_W), BF16),
                   jax.ShapeDtypeStruct((D_FF, D_MODEL), BF16)],
        compiler_params=_params(("arbitrary", "arbitrary", "arbitrary")),
        name="swa_attn",
    )(slopes, sinks, proj, proj, proj, proj, proj, proj, proj, proj, pos_col, pos_row, pos_row, w_down)


def _oproj_a_kernel(oa_ref, ga_ref, w_ref, y_ref):
    oa = oa_ref[...].astype(F32)
    r = _row_rsqrt(oa)
    og = (oa * ga_ref[...]).astype(BF16)
    for cs in _col_chunks(D_MODEL):
        y_ref[:, cs] = jnp.dot(og, w_ref[:, cs], preferred_element_type=F32) * r


def _out_proj_a(o_a, ga, w_oa, tm=512):
    T = o_a.shape[0]
    row = lambda i: (i, 0)
    return pl.pallas_call(
        _oproj_a_kernel,
        grid=(T // tm,),
        in_specs=[pl.BlockSpec((tm, MIX_A), row), _resident((1, MIX_A)), _resident((MIX_A, D_MODEL))],
        out_specs=pl.BlockSpec((tm, D_MODEL), row),
        out_shape=jax.ShapeDtypeStruct((T, D_MODEL), F32),
        compiler_params=_params(("parallel",)),
        name="out_proj_a",
    )(o_a, ga, w_oa)


def _oproj_b_kernel(ob_ref, gb_ref, w_ref, ya_ref, x_ref, gp_ref, h_ref):
    ob = jnp.concatenate([ob_ref[g] for g in range(SWA_GROUP)], axis=1).astype(F32)
    r = _row_rsqrt(ob)
    og = (ob * gb_ref[...]).astype(BF16)
    ssq = jnp.zeros((h_ref.shape[0], 1), F32)
    for cs in _col_chunks(D_MODEL):
        y = ya_ref[:, cs] + jnp.dot(og, w_ref[:, cs], preferred_element_type=F32) * r
        ssq = ssq + jnp.sum(y * y, axis=-1, keepdims=True)
        h_ref[:, cs] = y
    rn = lax.rsqrt(ssq * (1.0 / D_MODEL) + EPS)
    for cs in _col_chunks(D_MODEL):
        h_ref[:, cs] = x_ref[:, cs] + h_ref[:, cs] * rn * gp_ref[:, cs]


def _out_proj_b(o_b, gb, w_o_bf, y_a, x2, gpost, tm=256):
    T = x2.shape[0]
    row = lambda i: (i, 0)
    return pl.pallas_call(
        _oproj_b_kernel,
        grid=(T // tm,),
        in_specs=[
            pl.BlockSpec((SWA_GROUP, tm, KV_W), lambda i: (0, i, 0)),
            _resident((1, MIX_B)),
            _resident((MIX_B, D_MODEL), block=(MIX_A // MIX_B, 0)),
            pl.BlockSpec((tm, D_MODEL), row),
            pl.BlockSpec((tm, D_MODEL), row),
            _resident((1, D_MODEL)),
        ],
        out_specs=pl.BlockSpec((tm, D_MODEL), row),
        out_shape=jax.ShapeDtypeStruct((T, D_MODEL), F32),
        compiler_params=_params(("parallel",)),
        name="out_proj_b",
    )(o_b, gb, w_o_bf, y_a, x2, gpost)


def _ffn_kernel(h_ref, gpre_ref, wg_ref, wu_ref, wd_ref, gpost_ref, o_ref, f_ref, *, nj):
    j = pl.program_id(1)

    @pl.when(j == 0)
    def _():
        def norm_rows(rows):
            f_ref[rows, :] = _rms(h_ref[rows, :], gpre_ref[...]).astype(BF16)
            o_ref[rows, :] = jnp.zeros((FINISH_ROWS, D_MODEL), F32)
        _for_row_chunks(f_ref.shape[0], norm_rows, FINISH_ROWS)

    f = f_ref[...]
    gate = jnp.dot(f, wg_ref[...], preferred_element_type=F32)
    up = jnp.dot(f, wu_ref[...], preferred_element_type=F32)
    act = (gate * jax.nn.sigmoid(gate) * up).astype(BF16)
    for cs in _col_chunks(D_MODEL, DOWN_CHUNK):
        o_ref[:, cs] += jnp.dot(act, wd_ref[:, cs], preferred_element_type=F32)

    @pl.when(j == nj - 1)
    def _():
        def finish_rows(rows):
            o_ref[rows, :] = h_ref[rows, :] + _rms(o_ref[rows, :], gpost_ref[...])
        _for_row_chunks(o_ref.shape[0], finish_rows, FINISH_ROWS)


def _ffn(h, gpre, wg, wu, wd, gpost, tm=512, tf=FF_TILE):
    T = h.shape[0]
    nj = D_FF // tf
    return pl.pallas_call(
        functools.partial(_ffn_kernel, nj=nj),
        grid=(T // tm, nj),
        in_specs=[
            pl.BlockSpec((tm, D_MODEL), lambda i, j: (i, 0)),
            pl.BlockSpec((1, D_MODEL), lambda i, j: (0, 0)),
            pl.BlockSpec((D_MODEL, tf), lambda i, j: (0, j)),
            pl.BlockSpec((D_MODEL, tf), lambda i, j: (0, j)),
            pl.BlockSpec((tf, D_MODEL), lambda i, j: (j, 0)),
            pl.BlockSpec((1, D_MODEL), lambda i, j: (0, 0)),
        ],
        out_specs=pl.BlockSpec((tm, D_MODEL), lambda i, j: (i, 0)),
        out_shape=jax.ShapeDtypeStruct((T, D_MODEL), F32),
        scratch_shapes=[pltpu.VMEM((tm, D_MODEL), BF16)],
        compiler_params=_params(("parallel", "arbitrary")),
        name="ffn",
    )(h, gpre, wg, wu, wd, gpost)


def _w_in_prep_kernel(w_ref, lat_ref, rope_ref, q_ref, kv_ref):
    c0 = LATENT_W
    c1 = c0 + MLA_ROPE
    c2 = c1 + MIX_B
    half = MLA_ROPE // 2
    w = w_ref[...]
    lat_ref[...] = w[:, :c0].astype(BF16)
    rope_ref[...] = jnp.concatenate([w[:, c0:c1], w[:, c0 + half:c1], w[:, c0:c0 + half]], axis=1).astype(BF16)
    for g in range(SWA_GROUP):
        for pair in range(SWA_KV_HEADS // 2):
            srcs = [c1 + ((2 * pair + c) * SWA_GROUP + g) * SWA_HEAD_DIM for c in range(2)]
            dst = (g * SWA_KV_HEADS + 2 * pair) * SWA_HEAD_DIM
            q_ref[:, dst:dst + LANES] = jnp.concatenate(
                [w[:, a:a + SWA_HEAD_DIM] for a in srcs], axis=1).astype(BF16)
    kv_ref[...] = w[:, c2:].astype(BF16)


def _prep_w_in(w_in, layer, rows=128):
    row = lambda i: (i, 0)
    widths = (LATENT_W, LANES, MIX_B, 2 * KV_W)
    return pl.pallas_call(
        _w_in_prep_kernel,
        grid=(D_MODEL // rows,),
        in_specs=[pl.BlockSpec((None, rows, w_in.shape[2]), lambda i: (layer, i, 0))],
        out_specs=[pl.BlockSpec((rows, n), row) for n in widths],
        out_shape=[jax.ShapeDtypeStruct((D_MODEL, n), BF16) for n in widths],
        compiler_params=_params(("parallel",)),
        name="w_in_prep",
    )(w_in)


def _w_up_prep_kernel(wq_ref, wkv_ref, wq_o_ref, wkv_o_ref):
    half = MLA_ROPE // 2
    wq = wq_ref[...]
    wkv = wkv_ref[...]
    for h in range(MLA_HEADS):
        src = h * MLA_QK
        pe = src + MLA_NOPE
        wq_o_ref[:, h * QK_PAD:(h + 1) * QK_PAD] = jnp.concatenate(
            [wq[:, src:src + MLA_QK], wq[:, pe + half:pe + MLA_ROPE], wq[:, pe:pe + half]], axis=1).astype(BF16)
        kv_src = h * (MLA_NOPE + MLA_V)
        wkv_o_ref[:, h * MLA_NOPE:(h + 1) * MLA_NOPE] = wkv[:, kv_src:kv_src + MLA_NOPE].astype(BF16)
        wkv_o_ref[:, MIX_A + h * MLA_V:MIX_A + (h + 1) * MLA_V] = (
            wkv[:, kv_src + MLA_NOPE:kv_src + MLA_NOPE + MLA_V].astype(BF16))


def _prep_w_up(w_uq, w_ukv, steps=8):
    row = lambda i: (i, 0)
    rq, rkv = MLA_Q_RANK // steps, MLA_KV_RANK // steps
    nq, nkv = MLA_HEADS * QK_PAD, MLA_HEADS * (MLA_NOPE + MLA_V)
    return pl.pallas_call(
        _w_up_prep_kernel,
        grid=(steps,),
        in_specs=[pl.BlockSpec((rq, w_uq.shape[1]), row), pl.BlockSpec((rkv, w_ukv.shape[1]), row)],
        out_specs=[pl.BlockSpec((rq, nq), row), pl.BlockSpec((rkv, nkv), row)],
        out_shape=[jax.ShapeDtypeStruct((MLA_Q_RANK, nq), BF16), jax.ShapeDtypeStruct((MLA_KV_RANK, nkv), BF16)],
        compiler_params=_params(("parallel",)),
        name="w_up_prep",
    )(w_uq, w_ukv)


def _swa_perm_rows(a):
    rest = a.shape[1:]
    a = a.reshape((SWA_KV_HEADS, SWA_GROUP, SWA_HEAD_DIM) + rest)
    return jnp.swapaxes(a, 0, 1).reshape((MIX_B,) + rest)


def kernel(x, positions, attn_pre_g, w_in, q_norm_g, w_uq, kv_norm_g, w_ukv, swa_sinks, grp_a_g, grp_b_g,
           w_o, attn_post_g, ffn_pre_g, w_gate, w_up, w_down, ffn_post_g):
    B, S, _ = x.shape
    T = B * S
    depth = w_in.shape[0]
    inv = 1.0 / (ROPE_THETA ** (jnp.arange(0, MLA_ROPE, 2, dtype=F32) / MLA_ROPE))
    zeros64 = jnp.zeros((MLA_ROPE,), F32)
    inv_l = jnp.concatenate([inv, inv, zeros64]).reshape(1, LANES)
    cmask = jnp.concatenate([jnp.ones((MLA_ROPE,), F32), zeros64]).reshape(1, LANES)
    half = MLA_ROPE // 2
    sgn = jnp.concatenate([-jnp.ones((half,), F32), jnp.ones((half,), F32), zeros64]).reshape(1, LANES)
    slopes = jnp.exp2(-8.0 * jnp.arange(1, SWA_Q_HEADS + 1, dtype=F32) / SWA_Q_HEADS)
    pos_col = positions.reshape(T, 1)
    pos_row = positions.reshape(B, 1, S)

    h = x.reshape(T, D_MODEL)
    for l in range(depth):
        w_lat, w_rope, w_q, w_kv = _prep_w_in(w_in, l)
        w_uq_r, w_ukv_r = _prep_w_up(w_uq[l], w_ukv[l])
        gb = _swa_perm_rows(grp_b_g[l]).reshape(1, MIX_B)
        g_pre = attn_pre_g[l].reshape(1, D_MODEL)

        lat, kr = _in_proj_mla(h, g_pre, w_lat, w_rope)
        proj_swa, w_o_bf = _in_proj_swa(h, g_pre, w_q, w_kv, w_o[l])
        q, kv, kpe = _qkv_up(lat, kr, pos_col, inv_l, cmask, sgn, q_norm_g[l].reshape(1, MLA_Q_RANK),
                             kv_norm_g[l].reshape(1, MLA_KV_RANK), w_uq_r, w_ukv_r)
        o_a, wg_bf, wu_bf = _mla_attention(q, kv, kpe, w_gate[l], w_up[l], B, S)
        y_a = _out_proj_a(o_a, grp_a_g[l].reshape(1, MIX_A), w_o_bf)
        o_b, wd_bf = _swa_attention(proj_swa, pos_col, pos_row, slopes, swa_sinks[l].astype(F32),
                                    w_down[l], B, S)
        h = _out_proj_b(o_b, gb, w_o_bf, y_a, h, attn_post_g[l].reshape(1, D_MODEL))
        h = _ffn(h, ffn_pre_g[l].reshape(1, D_MODEL), wg_bf, wu_bf, wd_bf, ffn_post_g[l].reshape(1, D_MODEL))
    return h.reshape(B, S, D_MODEL)
```

```python
import functools

import jax
import jax.numpy as jnp
from jax import lax
from jax.experimental import pallas as pl
from jax.experimental.pallas import tpu as pltpu

D_MODEL = 4096
MLA_HEADS = 16
MLA_Q_RANK = 1024
MLA_KV_RANK = 512
MLA_NOPE = 128
MLA_ROPE = 64
MLA_V = 128
MLA_QK = MLA_NOPE + MLA_ROPE
ROPE_THETA = 10000.0
SWA_Q_HEADS = 32
SWA_KV_HEADS = 8
SWA_HEAD_DIM = 64
SWA_GROUP = SWA_Q_HEADS // SWA_KV_HEADS
SWA_WINDOW = 128
BLOCK = 128
MIX_A = MLA_HEADS * MLA_V
MIX_B = SWA_Q_HEADS * SWA_HEAD_DIM
D_FF = 11008
EPS = 1e-6

LANES = 128
QK_PAD = 256
KV_W = SWA_KV_HEADS * SWA_HEAD_DIM
LATENT_W = MLA_Q_RANK + MLA_KV_RANK
COL_QS = 0
COL_KS = COL_QS + MIX_B
COL_VS = COL_KS + KV_W
SWA_IN_W = COL_VS + KV_W
NEG = float(jnp.finfo(jnp.float32).min)
VMEM_LIMIT = 56 * 1024 * 1024
DOWN_CHUNK = 1024
FF_TILE = 256
N_CHUNK = 1024
FINISH_ROWS = 64
LOG2_E = 1.4426950408889634
MASKED_DIST = 1e30
F32 = jnp.float32
BF16 = jnp.bfloat16


def _rms(xf, g):
    return xf * lax.rsqrt(jnp.mean(xf * xf, axis=-1, keepdims=True) + EPS) * g


def _row_rsqrt(xf):
    return lax.rsqrt(jnp.mean(xf * xf, axis=-1, keepdims=True) + EPS)


def _for_row_chunks(n_rows, fn, chunk, unroll=1):
    def body(r, carry):
        fn(pl.ds(pl.multiple_of(r * chunk, chunk), chunk))
        return carry
    lax.fori_loop(0, n_rows // chunk, body, 0, unroll=unroll)


def _params(sem):
    return pltpu.CompilerParams(dimension_semantics=sem, vmem_limit_bytes=VMEM_LIMIT)


def _resident(shape, block=None):
    block = (0,) * len(shape) if block is None else block
    return pl.BlockSpec(shape, lambda *_: block, pipeline_mode=pl.Buffered(1))


def _col_chunks(width, chunk=N_CHUNK):
    return [slice(lo, min(lo + chunk, width)) for lo in range(0, width, chunk)]


def _inproj_mla_kernel(x_ref, g_ref, wl_ref, wr_ref, lat_ref, kr_ref):
    x = x_ref[...]
    r = _row_rsqrt(x)
    xg = (x * g_ref[...]).astype(BF16)
    lat_ref[...] = (jnp.dot(xg, wl_ref[...], preferred_element_type=F32) * r).astype(BF16)
    kr_ref[...] = jnp.dot(xg, wr_ref[...], preferred_element_type=F32) * r


def _in_proj_mla(x2, g, w_lat, w_rope, tm=512):
    T = x2.shape[0]
    row = lambda i: (i, 0)
    return pl.pallas_call(
        _inproj_mla_kernel,
        grid=(T // tm,),
        in_specs=[
            pl.BlockSpec((tm, D_MODEL), row),
            _resident((1, D_MODEL)),
            _resident((D_MODEL, LATENT_W)),
            _resident((D_MODEL, LANES)),
        ],
        out_specs=[pl.BlockSpec((tm, LATENT_W), row), pl.BlockSpec((tm, LANES), row)],
        out_shape=[jax.ShapeDtypeStruct((T, LATENT_W), BF16), jax.ShapeDtypeStruct((T, LANES), F32)],
        compiler_params=_params(("parallel",)),
        name="in_proj_mla",
    )(x2, g, w_lat, w_rope)


def _inproj_swa_kernel(x_ref, g_ref, wq_ref, wkv_ref, wo_ref, o_ref, wo_bf_ref):
    wo_bf_ref[...] = wo_ref[...].astype(BF16)
    x = x_ref[...]
    r = _row_rsqrt(x)
    xg = (x * g_ref[...]).astype(BF16)
    rq = r * (SWA_HEAD_DIM ** -0.5 * LOG2_E)
    for cs in _col_chunks(MIX_B):
        o_ref[:, cs] = (jnp.dot(xg, wq_ref[:, cs], preferred_element_type=F32) * rq).astype(BF16)
    o_ref[:, COL_KS:] = (jnp.dot(xg, wkv_ref[...], preferred_element_type=F32) * r).astype(BF16)


def _in_proj_swa(x2, g, w_q, w_kv, w_o):
    T = x2.shape[0]
    n_slabs = (MIX_A + MIX_B) // SWA_HEAD_DIM
    n_top = MIX_A // SWA_HEAD_DIM
    tm = T // n_slabs
    row = lambda i: (i, 0)

    def slab_dst(i):
        r = jnp.maximum(i - n_top, 0)
        return jnp.where(i < n_top, i, n_top + (r % SWA_GROUP) * SWA_KV_HEADS + r // SWA_GROUP), 0

    return pl.pallas_call(
        _inproj_swa_kernel,
        grid=(n_slabs,),
        in_specs=[
            pl.BlockSpec((tm, D_MODEL), row),
            _resident((1, D_MODEL)),
            _resident((D_MODEL, MIX_B)),
            _resident((D_MODEL, 2 * KV_W)),
            pl.BlockSpec((SWA_HEAD_DIM, D_MODEL), row),
        ],
        out_specs=[pl.BlockSpec((tm, SWA_IN_W), row), pl.BlockSpec((SWA_HEAD_DIM, D_MODEL), slab_dst)],
        out_shape=[jax.ShapeDtypeStruct((T, SWA_IN_W), BF16),
                   jax.ShapeDtypeStruct((MIX_A + MIX_B, D_MODEL), BF16)],
        compiler_params=_params(("arbitrary",)),
        name="in_proj_swa",
    )(x2, g, w_q, w_kv, w_o)


def _qkvup_kernel(cq_ref, ckv_ref, kr_ref, pos_ref, inv_ref, cmask_ref, sgn_ref, gq_ref, gkv_ref,
                  wq_ref, wkv_ref, q_ref, kv_ref, kpe_ref):
    ang = pos_ref[...].astype(F32) * inv_ref[...]
    c = jnp.cos(ang) * cmask_ref[...]
    s = jnp.sin(ang) * sgn_ref[...]
    kr = kr_ref[...]
    kpe_ref[...] = (kr * c + pltpu.roll(kr, 64, 1) * s).astype(BF16)
    scale = MLA_QK ** -0.5 * LOG2_E
    cq = cq_ref[...].astype(F32)
    rq = _row_rsqrt(cq) * scale
    cqg = (cq * gq_ref[...]).astype(BF16)
    for cs in _col_chunks(MLA_HEADS * QK_PAD):
        y = jnp.dot(cqg, wq_ref[:, cs], preferred_element_type=F32) * rq
        for h in range((cs.stop - cs.start) // QK_PAD):
            lo = h * QK_PAD
            q_ref[:, cs.start + lo:cs.start + lo + MLA_NOPE] = y[:, lo:lo + MLA_NOPE].astype(BF16)
            pe = y[:, lo + MLA_NOPE:lo + QK_PAD]
            q_ref[:, cs.start + lo + MLA_NOPE:cs.start + lo + QK_PAD] = (
                pe * c + pltpu.roll(pe, 64, 1) * s).astype(BF16)
    ckv = ckv_ref[...].astype(F32)
    rkv = _row_rsqrt(ckv)
    ckvg = (ckv * gkv_ref[...]).astype(BF16)
    for cs in _col_chunks(MLA_HEADS * (MLA_NOPE + MLA_V)):
        kv_ref[:, cs] = (jnp.dot(ckvg, wkv_ref[:, cs], preferred_element_type=F32) * rkv).astype(BF16)


def _qkv_up(lat, kr, pos_col, inv, cmask, sgn, gq, gkv, w_uq_r, w_ukv_r, tm=512):
    T = lat.shape[0]
    nq = MLA_HEADS * QK_PAD
    nkv = MLA_HEADS * (MLA_NOPE + MLA_V)
    row = lambda i: (i, 0)
    return pl.pallas_call(
        _qkvup_kernel,
        grid=(T // tm,),
        in_specs=[
            pl.BlockSpec((tm, MLA_Q_RANK), row),
            pl.BlockSpec((tm, MLA_KV_RANK), lambda i: (i, MLA_Q_RANK // MLA_KV_RANK)),
            pl.BlockSpec((tm, LANES), row),
            pl.BlockSpec((tm, 1), row),
            _resident((1, LANES)),
            _resident((1, LANES)),
            _resident((1, LANES)),
            _resident((1, MLA_Q_RANK)),
            _resident((1, MLA_KV_RANK)),
            _resident((MLA_Q_RANK, nq)),
            _resident((MLA_KV_RANK, nkv)),
        ],
        out_specs=[pl.BlockSpec((tm, nq), row), pl.BlockSpec((tm, nkv), row), pl.BlockSpec((tm, LANES), row)],
        out_shape=[jax.ShapeDtypeStruct((T, nq), BF16), jax.ShapeDtypeStruct((T, nkv), BF16),
                   jax.ShapeDtypeStruct((T, LANES), BF16)],
        compiler_params=_params(("parallel",)),
        name="qkv_up",
    )(lat, lat, kr, pos_col, inv, cmask, sgn, gq, gkv, w_uq_r, w_ukv_r)


def _mla_kernel(q_ref, kn_ref, kpe_ref, v_ref, wg_ref, wu_ref, o_ref, wg_bf_ref, wu_bf_ref,
                m_ref, l_ref, acc_ref, *, th, heads):
    wg_bf_ref[...] = wg_ref[...].astype(BF16)
    wu_bf_ref[...] = wu_ref[...].astype(BF16)
    qi = pl.program_id(2)
    m_ref[...] = jnp.full(m_ref.shape, NEG, F32)
    l_ref[...] = jnp.zeros(l_ref.shape, F32)
    acc_ref[...] = jnp.zeros(acc_ref.shape, F32)

    def load_kv(head, start, size):
        rows = pl.ds(pl.multiple_of(start, size), size)
        k = jnp.concatenate([kn_ref[rows, head * MLA_NOPE:(head + 1) * MLA_NOPE], kpe_ref[rows, :]], axis=1)
        return k, v_ref[rows, head * MLA_V:(head + 1) * MLA_V]

    def update(head, half, k, v, diag_col=None):
        if half is None:
            rows, slots, n = slice(0, 2 * th), slice(2 * head, 2 * head + 2), 2 * th
        else:
            rows, slots, n = slice(half * th, (half + 1) * th), slice(2 * head + half, 2 * head + half + 1), th
        q = q_ref[rows, head * QK_PAD:(head + 1) * QK_PAD]
        s = lax.dot_general(q, k, (((1,), (1,)), ((), ())), preferred_element_type=F32)
        chunks = [s[:, c * LANES:(c + 1) * LANES] for c in range(s.shape[1] // LANES)]
        if diag_col is not None:
            tri = (lax.broadcasted_iota(jnp.int32, (LANES, LANES), 1)
                   <= lax.broadcasted_iota(jnp.int32, (LANES, LANES), 0))
            for c in range(diag_col // LANES, len(chunks)):
                r0 = c * LANES - diag_col
                parts = [jnp.full((r0, LANES), NEG, F32)] if r0 else []
                parts.append(jnp.where(tri, chunks[c][r0:r0 + LANES], NEG))
                if r0 + LANES < n:
                    parts.append(chunks[c][r0 + LANES:])
                chunks[c] = jnp.concatenate(parts, axis=0)
        mx = chunks[0]
        for c in chunks[1:]:
            mx = jnp.maximum(mx, c)
        m_prev = m_ref[slots].reshape(n, LANES)
        m_new = jnp.maximum(m_prev, jnp.max(mx, axis=-1, keepdims=True))
        alpha = jnp.exp2(m_prev - m_new)
        ps = []
        for c, chunk in enumerate(chunks):
            r0 = max(c * LANES - diag_col, 0) if diag_col is not None else 0
            p = jnp.exp2(chunk[r0:] - m_new[r0:])
            ps.append(jnp.concatenate([jnp.zeros((r0, LANES), F32), p], axis=0) if r0 else p)
        psum = ps[0]
        for p in ps[1:]:
            psum = psum + p
        l_ref[slots] = (alpha * l_ref[slots].reshape(n, LANES) + psum).reshape(-1, th, LANES)
        p = jnp.concatenate(ps, axis=1).astype(BF16)
        acc = alpha * acc_ref[slots].reshape(n, MLA_V) + jnp.dot(p, v, preferred_element_type=F32)
        acc_ref[slots] = acc.reshape(-1, th, MLA_V)
        m_ref[slots] = m_new.reshape(-1, th, LANES)

    def body(j, carry):
        for head in range(heads):
            k, v = load_kv(head, j * tq, tq)
            update(head, None, k, v)
        return carry

    tq = 2 * th
    lax.fori_loop(0, qi, body, 0)
    for head in range(heads):
        k, v = load_kv(head, qi * tq, tq)
        update(head, 0, k[:th], v[:th], diag_col=0)
        update(head, 1, k, v, diag_col=th)
    for head in range(heads):
        for half in range(2):
            l = jnp.sum(l_ref[2 * head + half], axis=-1, keepdims=True)
            o_ref[half * th:(half + 1) * th, head * MLA_V:(head + 1) * MLA_V] = (
                acc_ref[2 * head + half] / l).astype(BF16)


def _mla_attention(q, kv, kpe, w_gate, w_up, B, S, th=512, heads=4):
    T = B * S
    tq = 2 * th
    nq = S // tq
    n_groups = MLA_HEADS // heads
    slab = D_MODEL // (B * n_groups * nq)
    w_spec = pl.BlockSpec((slab, D_FF), lambda b, h, i: ((b * n_groups + h) * nq + i, 0))
    return pl.pallas_call(
        functools.partial(_mla_kernel, th=th, heads=heads),
        grid=(B, n_groups, nq),
        in_specs=[
            pl.BlockSpec((tq, heads * QK_PAD), lambda b, h, i: (b * nq + i, h)),
            pl.BlockSpec((S, heads * MLA_NOPE), lambda b, h, i: (b, h)),
            pl.BlockSpec((S, LANES), lambda b, h, i: (b, 0)),
            pl.BlockSpec((S, heads * MLA_V), lambda b, h, i: (b, n_groups + h)),
            w_spec, w_spec,
        ],
        out_specs=[pl.BlockSpec((tq, heads * MLA_V), lambda b, h, i: (b * nq + i, h)), w_spec, w_spec],
        out_shape=[jax.ShapeDtypeStruct((T, MIX_A), BF16),
                   jax.ShapeDtypeStruct((D_MODEL, D_FF), BF16),
                   jax.ShapeDtypeStruct((D_MODEL, D_FF), BF16)],
        scratch_shapes=[
            pltpu.VMEM((2 * heads, th, LANES), F32),
            pltpu.VMEM((2 * heads, th, LANES), F32),
            pltpu.VMEM((2 * heads, th, MLA_V), F32),
        ],
        compiler_params=_params(("arbitrary", "arbitrary", "arbitrary")),
        name="mla_attn",
    )(q, kv, kpe, kv, w_gate, w_up)


def _swa_kernel(slope_ref, sink_ref, q0_ref, q1_ref, q2_ref, q3_ref, k_ref, kp_ref, v_ref, vp_ref,
                pc_ref, pr_ref, prp_ref, wd_ref, o_ref, wd_bf_ref, *, tq):
    wd_bf_ref[...] = wd_ref[...].astype(BF16)
    n = pl.program_id(1)
    pair = pl.program_id(2)
    q_refs = (q0_ref, q1_ref, q2_ref, q3_ref)
    lane = lax.broadcasted_iota(jnp.int32, (2 * BLOCK, LANES), 1)
    lo = lane < SWA_HEAD_DIM
    k_all = jnp.concatenate([kp_ref[...], k_ref[...]], axis=0)
    v_all = jnp.concatenate([vp_ref[...], v_ref[...]], axis=0)
    pos_k = jnp.concatenate([prp_ref[0], pr_ref[0]], axis=1)
    i_idx = lax.broadcasted_iota(jnp.int32, (BLOCK, 2 * BLOCK), 0)
    c_idx = lax.broadcasted_iota(jnp.int32, (BLOCK, 2 * BLOCK), 1)
    in_window = (c_idx > i_idx) & (c_idx <= i_idx + SWA_WINDOW)
    zero = jnp.zeros((), BF16)
    lo_out = lax.broadcasted_iota(jnp.int32, (BLOCK, LANES), 1) < SWA_HEAD_DIM

    for t in range(tq // BLOCK):
        r0 = t * BLOCK
        band_k = k_all[r0:r0 + 2 * BLOCK]
        band_v = v_all[r0:r0 + 2 * BLOCK]
        k_bd = jnp.concatenate([jnp.where(lo, band_k, zero), jnp.where(lo, zero, band_k)], axis=0)
        v_bd = jnp.concatenate([jnp.where(lo, band_v, zero), jnp.where(lo, zero, band_v)], axis=0)
        q_st = jnp.concatenate([qr[r0:r0 + BLOCK, :] for qr in q_refs], axis=0)
        s = lax.dot_general(q_st, k_bd, (((1,), (1,)), ((), ())), preferred_element_type=F32)
        dist = jnp.abs(pc_ref[r0:r0 + BLOCK, :] - pos_k[:, r0:r0 + 2 * BLOCK]).astype(F32)
        valid = in_window
        if t == 0:
            valid = valid & ((c_idx >= BLOCK) | (n > 0))
        dist = jnp.where(valid, dist, MASKED_DIST)
        rows = []
        inv_den = []
        for g in range(SWA_GROUP):
            halves = []
            for c in range(2):
                head = (2 * pair + c) * SWA_GROUP + g
                slope = slope_ref[head] * LOG2_E
                sink = sink_ref[head] * LOG2_E
                sg = s[g * BLOCK:(g + 1) * BLOCK, c * 2 * BLOCK:(c + 1) * 2 * BLOCK] - slope * dist
                m = jnp.maximum(jnp.max(sg, axis=-1, keepdims=True), sink)
                e = jnp.exp2(sg - m)
                den = jnp.sum(e, axis=-1, keepdims=True) + jnp.exp2(sink - m)
                halves.append(e.astype(BF16))
                inv_den.append(1.0 / den)
            rows.append(jnp.concatenate(halves, axis=1))
        p = jnp.concatenate(rows, axis=0)
        out = jnp.dot(p, v_bd, preferred_element_type=F32)
        for g in range(SWA_GROUP):
            norm = jnp.where(lo_out, inv_den[2 * g], inv_den[2 * g + 1])
            o_ref[g, r0:r0 + BLOCK, :] = (out[g * BLOCK:(g + 1) * BLOCK] * norm).astype(BF16)


def _swa_attention(proj, pos_col, pos_row, slopes, sinks, w_down, B, S, tq=1024):
    T = B * S
    nt = S // tq
    sub = tq // BLOCK
    npairs = SWA_KV_HEADS // 2
    n_steps = B * nt * npairs
    slab = FF_TILE if n_steps >= D_FF // FF_TILE else D_FF
    last_slab = D_FF // slab - 1
    wd_spec = pl.BlockSpec(
        (slab, D_MODEL), lambda b, n, j: (jnp.minimum((b * nt + n) * npairs + j, last_slab), 0))
    qs0 = COL_QS // LANES
    ks0 = COL_KS // LANES
    vs0 = COL_VS // LANES
    pairs_per_group = KV_W // LANES

    def q_spec(g):
        return pl.BlockSpec((tq, LANES), lambda b, n, j, g=g: (b * nt + n, qs0 + pairs_per_group * g + j))

    def prev_row(b, n):
        return jnp.maximum(b * (S // BLOCK) + sub * n - 1, 0)

    smem = pl.BlockSpec(memory_space=pltpu.SMEM)
    return pl.pallas_call(
        functools.partial(_swa_kernel, tq=tq),
        grid=(B, nt, npairs),
        in_specs=[
            smem, smem,
            q_spec(0), q_spec(1), q_spec(2), q_spec(3),
            pl.BlockSpec((tq, LANES), lambda b, n, j: (b * nt + n, ks0 + j)),
            pl.BlockSpec((BLOCK, LANES), lambda b, n, j: (prev_row(b, n), ks0 + j)),
            pl.BlockSpec((tq, LANES), lambda b, n, j: (b * nt + n, vs0 + j)),
            pl.BlockSpec((BLOCK, LANES), lambda b, n, j: (prev_row(b, n), vs0 + j)),
            pl.BlockSpec((tq, 1), lambda b, n, j: (b * nt + n, 0)),
            pl.BlockSpec((1, 1, tq), lambda b, n, j: (b, 0, n)),
            pl.BlockSpec((1, 1, BLOCK), lambda b, n, j: (b, 0, jnp.maximum(sub * n - 1, 0))),
            wd_spec,
        ],
        out_specs=[pl.BlockSpec((SWA_GROUP, tq, LANES), lambda b, n, j: (0, b * nt + n, j)), wd_spec],
        out_shape=[jax.ShapeDtypeStruct((SWA_GROUP, T, KV_W), BF16),
                   jax.ShapeDtypeStruct((D_FF, D_MODEL), BF16)],
        compiler_params=_params(("arbitrary", "arbitrary", "arbitrary")),
        name="swa_attn",
    )(slopes, sinks, proj, proj, proj, proj, proj, proj, proj, proj, pos_col, pos_row, pos_row, w_down)


def _oproj_a_kernel(oa_ref, ga_ref, w_ref, y_ref):
    oa = oa_ref[...].astype(F32)
    r = _row_rsqrt(oa)
    og = (oa * ga_ref[...]).astype(BF16)
    for cs in _col_chunks(D_MODEL):
        y_ref[:, cs] = jnp.dot(og, w_ref[:, cs], preferred_element_type=F32) * r


def _out_proj_a(o_a, ga, w_oa, tm=512):
    T = o_a.shape[0]
    row = lambda i: (i, 0)
    return pl.pallas_call(
        _oproj_a_kernel,
        grid=(T // tm,),
        in_specs=[pl.BlockSpec((tm, MIX_A), row), _resident((1, MIX_A)), _resident((MIX_A, D_MODEL))],
        out_specs=pl.BlockSpec((tm, D_MODEL), row),
        out_shape=jax.ShapeDtypeStruct((T, D_MODEL), F32),
        compiler_params=_params(("parallel",)),
        name="out_proj_a",
    )(o_a, ga, w_oa)


def _oproj_b_kernel(ob_ref, gb_ref, w_ref, ya_ref, x_ref, gp_ref, h_ref):
    ob = jnp.concatenate([ob_ref[g] for g in range(SWA_GROUP)], axis=1).astype(F32)
    r = _row_rsqrt(ob)
    og = (ob * gb_ref[...]).astype(BF16)
    ssq = jnp.zeros((h_ref.shape[0], 1), F32)
    for cs in _col_chunks(D_MODEL):
        y = ya_ref[:, cs] + jnp.dot(og, w_ref[:, cs], preferred_element_type=F32) * r
        ssq = ssq + jnp.sum(y * y, axis=-1, keepdims=True)
        h_ref[:, cs] = y
    rn = lax.rsqrt(ssq * (1.0 / D_MODEL) + EPS)
    for cs in _col_chunks(D_MODEL):
        h_ref[:, cs] = x_ref[:, cs] + h_ref[:, cs] * rn * gp_ref[:, cs]


def _out_proj_b(o_b, gb, w_o_bf, y_a, x2, gpost, tm=256):
    T = x2.shape[0]
    row = lambda i: (i, 0)
    return pl.pallas_call(
        _oproj_b_kernel,
        grid=(T // tm,),
        in_specs=[
            pl.BlockSpec((SWA_GROUP, tm, KV_W), lambda i: (0, i, 0)),
            _resident((1, MIX_B)),
            _resident((MIX_B, D_MODEL), block=(MIX_A // MIX_B, 0)),
            pl.BlockSpec((tm, D_MODEL), row),
            pl.BlockSpec((tm, D_MODEL), row),
            _resident((1, D_MODEL)),
        ],
        out_specs=pl.BlockSpec((tm, D_MODEL), row),
        out_shape=jax.ShapeDtypeStruct((T, D_MODEL), F32),
        compiler_params=_params(("parallel",)),
        name="out_proj_b",
    )(o_b, gb, w_o_bf, y_a, x2, gpost)


def _ffn_kernel(h_ref, gpre_ref, wg_ref, wu_ref, wd_ref, gpost_ref, o_ref, f_ref, *, nj):
    j = pl.program_id(1)

    @pl.when(j == 0)
    def _():
        def norm_rows(rows):
            f_ref[rows, :] = _rms(h_ref[rows, :], gpre_ref[...]).astype(BF16)
            o_ref[rows, :] = jnp.zeros((FINISH_ROWS, D_MODEL), F32)
        _for_row_chunks(f_ref.shape[0], norm_rows, FINISH_ROWS)

    f = f_ref[...]
    gate = jnp.dot(f, wg_ref[...], preferred_element_type=F32)
    up = jnp.dot(f, wu_ref[...], preferred_element_type=F32)
    act = (gate * jax.nn.sigmoid(gate) * up).astype(BF16)
    for cs in _col_chunks(D_MODEL, DOWN_CHUNK):
        o_ref[:, cs] += jnp.dot(act, wd_ref[:, cs], preferred_element_type=F32)

    @pl.when(j == nj - 1)
    def _():
        def finish_rows(rows):
            o_ref[rows, :] = h_ref[rows, :] + _rms(o_ref[rows, :], gpost_ref[...])
        _for_row_chunks(o_ref.shape[0], finish_rows, FINISH_ROWS)


def _ffn(h, gpre, wg, wu, wd, gpost, tm=512, tf=FF_TILE):
    T = h.shape[0]
    nj = D_FF // tf
    return pl.pallas_call(
        functools.partial(_ffn_kernel, nj=nj),
        grid=(T // tm, nj),
        in_specs=[
            pl.BlockSpec((tm, D_MODEL), lambda i, j: (i, 0)),
            pl.BlockSpec((1, D_MODEL), lambda i, j: (0, 0)),
            pl.BlockSpec((D_MODEL, tf), lambda i, j: (0, j)),
            pl.BlockSpec((D_MODEL, tf), lambda i, j: (0, j)),
            pl.BlockSpec((tf, D_MODEL), lambda i, j: (j, 0)),
            pl.BlockSpec((1, D_MODEL), lambda i, j: (0, 0)),
        ],
        out_specs=pl.BlockSpec((tm, D_MODEL), lambda i, j: (i, 0)),
        out_shape=jax.ShapeDtypeStruct((T, D_MODEL), F32),
        scratch_shapes=[pltpu.VMEM((tm, D_MODEL), BF16)],
        compiler_params=_params(("parallel", "arbitrary")),
        name="ffn",
    )(h, gpre, wg, wu, wd, gpost)


def _w_in_prep_kernel(w_ref, lat_ref, rope_ref, q_ref, kv_ref):
    c0 = LATENT_W
    c1 = c0 + MLA_ROPE
    c2 = c1 + MIX_B
    half = MLA_ROPE // 2
    w = w_ref[...]
    lat_ref[...] = w[:, :c0].astype(BF16)
    rope_ref[...] = jnp.concatenate([w[:, c0:c1], w[:, c0 + half:c1], w[:, c0:c0 + half]], axis=1).astype(BF16)
    for g in range(SWA_GROUP):
        for pair in range(SWA_KV_HEADS // 2):
            srcs = [c1 + ((2 * pair + c) * SWA_GROUP + g) * SWA_HEAD_DIM for c in range(2)]
            dst = (g * SWA_KV_HEADS + 2 * pair) * SWA_HEAD_DIM
            q_ref[:, dst:dst + LANES] = jnp.concatenate(
                [w[:, a:a + SWA_HEAD_DIM] for a in srcs], axis=1).astype(BF16)
    kv_ref[...] = w[:, c2:].astype(BF16)


def _prep_w_in(w_in, layer, rows=128):
    row = lambda i: (i, 0)
    widths = (LATENT_W, LANES, MIX_B, 2 * KV_W)
    return pl.pallas_call(
        _w_in_prep_kernel,
        grid=(D_MODEL // rows,),
        in_specs=[pl.BlockSpec((None, rows, w_in.shape[2]), lambda i: (layer, i, 0))],
        out_specs=[pl.BlockSpec((rows, n), row) for n in widths],
        out_shape=[jax.ShapeDtypeStruct((D_MODEL, n), BF16) for n in widths],
        compiler_params=_params(("parallel",)),
        name="w_in_prep",
    )(w_in)


def _w_up_prep_kernel(wq_ref, wkv_ref, wq_o_ref, wkv_o_ref):
    half = MLA_ROPE // 2
    wq = wq_ref[...]
    wkv = wkv_ref[...]
    for h in range(MLA_HEADS):
        src = h * MLA_QK
        pe = src + MLA_NOPE
        wq_o_ref[:, h * QK_PAD:(h + 1) * QK_PAD] = jnp.concatenate(
            [wq[:, src:src + MLA_QK], wq[:, pe + half:pe + MLA_ROPE], wq[:, pe:pe + half]], axis=1).astype(BF16)
        kv_src = h * (MLA_NOPE + MLA_V)
        wkv_o_ref[:, h * MLA_NOPE:(h + 1) * MLA_NOPE] = wkv[:, kv_src:kv_src + MLA_NOPE].astype(BF16)
        wkv_o_ref[:, MIX_A + h * MLA_V:MIX_A + (h + 1) * MLA_V] = (
            wkv[:, kv_src + MLA_NOPE:kv_src + MLA_NOPE + MLA_V].astype(BF16))


def _prep_w_up(w_uq, w_ukv, steps=8):
    row = lambda i: (i, 0)
    rq, rkv = MLA_Q_RANK // steps, MLA_KV_RANK // steps
    nq, nkv = MLA_HEADS * QK_PAD, MLA_HEADS * (MLA_NOPE + MLA_V)
    return pl.pallas_call(
        _w_up_prep_kernel,
        grid=(steps,),
        in_specs=[pl.BlockSpec((rq, w_uq.shape[1]), row), pl.BlockSpec((rkv, w_ukv.shape[1]), row)],
        out_specs=[pl.BlockSpec((rq, nq), row), pl.BlockSpec((rkv, nkv), row)],
        out_shape=[jax.ShapeDtypeStruct((MLA_Q_RANK, nq), BF16), jax.ShapeDtypeStruct((MLA_KV_RANK, nkv), BF16)],
        compiler_params=_params(("parallel",)),
        name="w_up_prep",
    )(w_uq, w_ukv)


def _swa_perm_rows(a):
    rest = a.shape[1:]
    a = a.reshape((SWA_KV_HEADS, SWA_GROUP, SWA_HEAD_DIM) + rest)
    return jnp.swapaxes(a, 0, 1).reshape((MIX_B,) + rest)


def kernel(x, positions, attn_pre_g, w_in, q_norm_g, w_uq, kv_norm_g, w_ukv, swa_sinks, grp_a_g, grp_b_g,
           w_o, attn_post_g, ffn_pre_g, w_gate, w_up, w_down, ffn_post_g):
    B, S, _ = x.shape
    T = B * S
    depth = w_in.shape[0]
    inv = 1.0 / (ROPE_THETA ** (jnp.arange(0, MLA_ROPE, 2, dtype=F32) / MLA_ROPE))
    zeros64 = jnp.zeros((MLA_ROPE,), F32)
    inv_l = jnp.concatenate([inv, inv, zeros64]).reshape(1, LANES)
    cmask = jnp.concatenate([jnp.ones((MLA_ROPE,), F32), zeros64]).reshape(1, LANES)
    half = MLA_ROPE // 2
    sgn = jnp.concatenate([-jnp.ones((half,), F32), jnp.ones((half,), F32), zeros64]).reshape(1, LANES)
    slopes = jnp.exp2(-8.0 * jnp.arange(1, SWA_Q_HEADS + 1, dtype=F32) / SWA_Q_HEADS)
    pos_col = positions.reshape(T, 1)
    pos_row = positions.reshape(B, 1, S)

    h = x.reshape(T, D_MODEL)
    for l in range(depth):
        w_lat, w_rope, w_q, w_kv = _prep_w_in(w_in, l)
        w_uq_r, w_ukv_r = _prep_w_up(w_uq[l], w_ukv[l])
        gb = _swa_perm_rows(grp_b_g[l]).reshape(1, MIX_B)
        g_pre = attn_pre_g[l].reshape(1, D_MODEL)

        lat, kr = _in_proj_mla(h, g_pre, w_lat, w_rope)
        proj_swa, w_o_bf = _in_proj_swa(h, g_pre, w_q, w_kv, w_o[l])
        q, kv, kpe = _qkv_up(lat, kr, pos_col, inv_l, cmask, sgn, q_norm_g[l].reshape(1, MLA_Q_RANK),
                             kv_norm_g[l].reshape(1, MLA_KV_RANK), w_uq_r, w_ukv_r)
        o_a, wg_bf, wu_bf = _mla_attention(q, kv, kpe, w_gate[l], w_up[l], B, S)
        y_a = _out_proj_a(o_a, grp_a_g[l].reshape(1, MIX_A), w_o_bf)
        o_b, wd_bf = _swa_attention(proj_swa, pos_col, pos_row, slopes, swa_sinks[l].astype(F32),
                                    w_down[l], B, S)
        h = _out_proj_b(o_b, gb, w_o_bf, y_a, h, attn_post_g[l].reshape(1, D_MODEL))
        h = _ffn(h, ffn_pre_g[l].reshape(1, D_MODEL), wg_bf, wu_bf, wd_bf, ffn_post_g[l].reshape(1, D_MODEL))
    return h.reshape(B, S, D_MODEL)
```
